```python
import math
import jax, jax.numpy as jnp
from jax import lax
import numpy as np

D_MODEL = 1024
BATCH = 8
SEQ = 4096
DEPTH = 4

GRID_W = 64
CTX_LEN = 256
N_MIXERS = 2
N_SSD = (DEPTH + 1) // 2
N_ATT = DEPTH // 2
RMS_EPS = 1e-6

SSD_EXPAND = 2
SSD_INNER = SSD_EXPAND * D_MODEL
SSD_HEADDIM = 64
SSD_HEADS = SSD_INNER // SSD_HEADDIM
SSD_GROUPS = 4
SSD_STATE = 128
SSD_CONV = 5
SSD_CHUNK = 128
SSD_CONV_DIM = SSD_INNER + 2 * SSD_GROUPS * SSD_STATE
SSD_IN_DIM = SSD_INNER + SSD_CONV_DIM + 2 * SSD_HEADS

DA_HEADS = D_MODEL // 128
DA_HEAD_DIM = 64
DA_V_DIM = 2 * DA_HEAD_DIM
DA_IN_DIM = DA_HEADS * (4 * DA_HEAD_DIM + DA_V_DIM)
Q_BLOCK = 128
ROPE_BASE = 10000.0

N_EXPERTS = 16
N_GROUPS = 4
EXPERTS_PER_GROUP = N_EXPERTS // N_GROUPS
TOP_K = 2
D_EXPERT = D_MODEL // 2

kernel_name = "hybrid_ssd_diffattn_grouped_moe_dit"

F32 = jnp.float32


def rmsnorm(x, g):
    xf = x.astype(F32)
    y = xf * lax.rsqrt(jnp.mean(xf * xf, axis=-1, keepdims=True) + RMS_EPS)
    return (y * g.astype(F32)).astype(x.dtype)


def modulate(h, shift, scale):
    return h * (1.0 + scale) + shift


def depthwise_conv_centred(u, w, b):
    k, ch = w.shape
    y = lax.conv_general_dilated(u, w.astype(u.dtype)[:, None, :], window_strides=(1,),
                                 padding=[(k // 2, k // 2)],
                                 dimension_numbers=("NWC", "WIO", "NWC"),
                                 feature_group_count=ch)
    return y + b.astype(u.dtype)


def ssd_chunked_scan(xs, a, bm, cm, h0):
    b, l, h, p = xs.shape
    g, n = bm.shape[2], bm.shape[3]
    r = h // g
    nch = l // SSD_CHUNK
    xs = xs.reshape(b, nch, SSD_CHUNK, g, r, p)
    bm = bm.reshape(b, nch, SSD_CHUNK, g, n)
    cm = cm.reshape(b, nch, SSD_CHUNK, g, n)
    a_cs = jnp.cumsum(a.reshape(b, nch, SSD_CHUNK, g, r), axis=2)
    a_t = a_cs.transpose(0, 1, 3, 4, 2)
    lower = jnp.tril(jnp.ones((SSD_CHUNK, SSD_CHUNK), dtype=bool))
    seg = jnp.exp(jnp.where(lower, a_t[..., :, None] - a_t[..., None, :], -jnp.inf))
    cb = jnp.einsum("bclgn,bcsgn->bcgls", cm, bm)
    y_diag = jnp.einsum("bcgrls,bcsgrp->bclgrp", cb[:, :, :, None] * seg, xs)
    to_end = jnp.exp(a_cs[:, :, -1:] - a_cs)
    chunk_states = jnp.einsum("bclgn,bclgrp->bcgrpn", bm, xs * to_end[..., None])
    chunk_decay = jnp.exp(a_cs[:, :, -1])

    def carry_state(state, inp):
        s_chunk, d_chunk = inp
        return state * d_chunk[..., None, None] + s_chunk, state

    h_last, h_start = lax.scan(carry_state, h0.reshape(b, g, r, p, n),
                               (jnp.moveaxis(chunk_states, 1, 0), jnp.moveaxis(chunk_decay, 1, 0)))
    h_start = jnp.moveaxis(h_start, 0, 1)
    y_off = jnp.einsum("bclgn,bcgrpn->bclgrp", cm, h_start) * jnp.exp(a_cs)[..., None]
    return (y_diag + y_off).reshape(b, l, h, p), h_last.reshape(b, h, p, n)


def ssd_mixer(hl, hc, w_in, conv_w, conv_b, dt_bias, a_log, d_skip, norm_g, w_out, need_ctx):
    def project(h):
        bsz, n = h.shape[0], h.shape[1]
        zxbcdt = h @ w_in
        z, xbc, dt = jnp.split(zxbcdt, [SSD_INNER, SSD_INNER + SSD_CONV_DIM], axis=-1)
        xbc = jax.nn.silu(depthwise_conv_centred(xbc, conv_w, conv_b))
        xs, bm, cm = jnp.split(xbc, [SSD_INNER, SSD_INNER + SSD_GROUPS * SSD_STATE], axis=-1)
        return (z, xs.reshape(bsz, n, SSD_HEADS, SSD_HEADDIM),
                bm.reshape(bsz, n, SSD_GROUPS, SSD_STATE), cm.reshape(bsz, n, SSD_GROUPS, SSD_STATE),
                dt.reshape(bsz, n, 2, SSD_HEADS))

    zl, xl, bl, cl, dtl = project(hl)
    zc, xc, bc, cc, dtc = project(hc)
    bsz = hl.shape[0]
    yl = (d_skip[:, None] * xl).astype(F32)
    yc = (d_skip[:, None] * xc).astype(F32)
    for k, rev in ((0, False), (1, True)):
        a_coef = -jnp.exp(a_log[k].astype(F32))

        def scan_inputs(xs, bm, cm, dt):
            dtk = jax.nn.softplus(dt[:, :, k].astype(F32) + dt_bias[k].astype(F32))
            seqs = (xs.astype(F32) * dtk[..., None], a_coef * dtk, bm.astype(F32), cm.astype(F32))
            return tuple(jnp.flip(s, axis=1) for s in seqs) if rev else seqs

        h0 = jnp.zeros((bsz, SSD_HEADS, SSD_HEADDIM, SSD_STATE), F32)
        y_c, h_ctx = ssd_chunked_scan(*scan_inputs(xc, bc, cc, dtc), h0)
        y_l, _ = ssd_chunked_scan(*scan_inputs(xl, bl, cl, dtl), h_ctx)
        if rev:
            y_c, y_l = jnp.flip(y_c, axis=1), jnp.flip(y_l, axis=1)
        yl = yl + y_l
        yc = yc + y_c

    def finish(y, z):
        y = y.astype(z.dtype).reshape(z.shape[0], z.shape[1], SSD_INNER)
        return rmsnorm(y * jax.nn.silu(z), norm_g) @ w_out

    return finish(yl, zl), (finish(yc, zc) if need_ctx else None)


def axial_rope_tables(n_tok, dtype):
    t = jnp.arange(n_tok, dtype=jnp.int32)
    pos = jnp.stack([t // GRID_W, t % GRID_W], axis=-1).astype(F32)
    n_freq = DA_HEAD_DIM // 4
    inv_freq = ROPE_BASE ** (-jnp.arange(n_freq, dtype=F32) / n_freq)
    ang = pos[..., None] * inv_freq
    return jnp.cos(ang).astype(dtype), jnp.sin(ang).astype(dtype)


def apply_axial_rope(x, cos, sin):
    b, n, m, h, d = x.shape
    xr = x.reshape(b, n, m, h, 2, 2, d // 4)
    x1, x2 = xr[..., 0, :], xr[..., 1, :]
    cs, sn = cos[:, None, None], sin[:, None, None]
    out = jnp.stack([x1 * cs - x2 * sn, x2 * cs + x1 * sn], axis=-2)
    return out.reshape(b, n, m, h, d)


def diff_softmax_attend(q, k, v, lam):
    s = jnp.einsum("bqmhd,bkmhd->bmhqk", q, k).astype(F32) * (DA_HEAD_DIM ** -0.5)
    p = jax.nn.softmax(s, axis=-1)
    att = p[:, 0] - lam * p[:, 1]
    return jnp.einsum("bhqk,bkhe->bqhe", att.astype(v.dtype), v)


def diff_attn_mixer(hl, hc, w_in, q_norm, k_norm, lam_q, lam_k, sub_norm, w_out, lam_init, need_ctx):
    bsz, n = hl.shape[0], hl.shape[1]
    lq, lk = lam_q.astype(F32), lam_k.astype(F32)
    lam = jnp.exp(jnp.sum(lq[0] * lk[0])) - jnp.exp(jnp.sum(lq[1] * lk[1])) + lam_init

    def project(h):
        qkv = h @ w_in
        q, k, v = jnp.split(qkv, [2 * DA_HEADS * DA_HEAD_DIM, 4 * DA_HEADS * DA_HEAD_DIM], axis=-1)
        q = rmsnorm(q.reshape(bsz, -1, 2, DA_HEADS, DA_HEAD_DIM), q_norm[:, None, :])
        k = rmsnorm(k.reshape(bsz, -1, 2, DA_HEADS, DA_HEAD_DIM), k_norm[:, None, :])
        return q, k, v.reshape(bsz, -1, DA_HEADS, DA_V_DIM)

    ql, kl, vl = project(hl)
    cos, sin = axial_rope_tables(n, hl.dtype)
    ql = apply_axial_rope(ql, cos, sin)
    kl = apply_axial_rope(kl, cos, sin)
    qc, kc, vc = project(hc)
    k_all = jnp.concatenate([kc, kl], axis=1)
    v_all = jnp.concatenate([vc, vl], axis=1)
    nb = n // Q_BLOCK
    q_blocks = ql.reshape(bsz, nb, Q_BLOCK, 2, DA_HEADS, DA_HEAD_DIM).transpose(1, 0, 2, 3, 4, 5)
    o = lax.map(lambda qb: diff_softmax_attend(qb, k_all, v_all, lam), q_blocks)
    o = o.transpose(1, 0, 2, 3, 4).reshape(bsz, n, DA_HEADS, DA_V_DIM)

    def finish(o):
        o = rmsnorm(o, sub_norm) * (1.0 - lam_init)
        return o.reshape(o.shape[0], o.shape[1], DA_HEADS * DA_V_DIM) @ w_out

    yc = finish(diff_softmax_attend(qc, kc, vc, lam)) if need_ctx else None
    return finish(o), yc


def grouped_moe(h, router_w, router_b, w_gate, w_up, w_down):
    shp = h.shape
    hf = h.reshape(-1, shp[-1])
    t = hf.shape[0]
    scores = jax.nn.sigmoid((hf @ router_w).astype(F32))
    biased = scores + router_b.astype(F32)
    grp = biased.reshape(t, N_GROUPS, EXPERTS_PER_GROUP)
    grp_score = jnp.sum(lax.top_k(grp, 2)[0], axis=-1)
    gsel = jax.nn.one_hot(jnp.argmax(grp_score, axis=-1), N_GROUPS, dtype=F32)
    masked = jnp.where(gsel[..., None] > 0, grp, -jnp.inf).reshape(t, N_EXPERTS)
    _, idx = lax.top_k(masked, TOP_K)
    wsel = jnp.take_along_axis(scores, idx, axis=-1)
    wsel = wsel / jnp.sum(wsel, axis=-1, keepdims=True)
    combine = jnp.sum(jax.nn.one_hot(idx, N_EXPERTS, dtype=F32) * wsel[..., None], axis=1).astype(h.dtype)
    out = jnp.zeros_like(hf)
    for e in range(N_EXPERTS):
        u = jax.nn.silu(hf @ w_gate[e]) * (hf @ w_up[e])
        out = out + combine[:, e:e + 1] * (u @ w_down[e])
    return out.reshape(shp)


def setup_inputs(seed: int = 0) -> dict:
    key = jax.random.key(seed)
    ks = iter(jax.random.split(key, 40))

    def nrm(shape, scale):
        return jax.random.normal(next(ks), shape, F32) * scale

    x = nrm((BATCH, SEQ, D_MODEL), 1.0)
    c = nrm((BATCH, D_MODEL), 1.0)
    ctx = nrm((BATCH, CTX_LEN, D_MODEL), 1.0)
    c_ctx = nrm((D_MODEL,), 1.0)
    ada_w = nrm((DEPTH, D_MODEL, 6 * D_MODEL), 0.5 * D_MODEL ** -0.5)
    ada_b = nrm((DEPTH, 6 * D_MODEL), 0.02)
    norm1_g = 1.0 + nrm((DEPTH, D_MODEL), 0.02)
    norm2_g = 1.0 + nrm((DEPTH, D_MODEL), 0.02)
    ssd_w_in = nrm((N_SSD, D_MODEL, SSD_IN_DIM), D_MODEL ** -0.5)
    ssd_conv_w = nrm((N_SSD, SSD_CONV, SSD_CONV_DIM), SSD_CONV ** -0.5)
    ssd_conv_b = nrm((N_SSD, SSD_CONV_DIM), 0.02)
    dt0 = jnp.exp(jax.random.uniform(next(ks), (N_SSD, 2, SSD_HEADS), F32, math.log(1e-3), math.log(1e-1)))
    ssd_dt_bias = dt0 + jnp.log(-jnp.expm1(-dt0))
    ssd_a_log = jnp.log(jax.random.uniform(next(ks), (N_SSD, 2, SSD_HEADS), F32, 1.0, 16.0))
    ssd_d = 1.0 + nrm((N_SSD, SSD_HEADS), 0.02)
    ssd_norm_g = 1.0 + nrm((N_SSD, SSD_INNER), 0.02)
    ssd_w_out = nrm((N_SSD, SSD_INNER, D_MODEL), SSD_INNER ** -0.5)
    da_w_in = nrm((N_ATT, D_MODEL, DA_IN_DIM), D_MODEL ** -0.5)
    da_q_norm = 1.0 + nrm((N_ATT, 2, DA_HEAD_DIM), 0.02)
    da_k_norm = 1.0 + nrm((N_ATT, 2, DA_HEAD_DIM), 0.02)
    da_lam_q = nrm((N_ATT, 2, DA_HEAD_DIM), 0.1)
    da_lam_k = nrm((N_ATT, 2, DA_HEAD_DIM), 0.1)
    da_sub_norm = 1.0 + nrm((N_ATT, DA_V_DIM), 0.02)
    da_w_out = nrm((N_ATT, DA_HEADS * DA_V_DIM, D_MODEL), (DA_HEADS * DA_V_DIM) ** -0.5)
    router_w = nrm((D_MODEL, N_EXPERTS), D_MODEL ** -0.5)
    router_b = nrm((N_EXPERTS,), 0.01)
    moe_w_gate = nrm((DEPTH, N_EXPERTS, D_MODEL, D_EXPERT), D_MODEL ** -0.5)
    moe_w_up = nrm((DEPTH, N_EXPERTS, D_MODEL, D_EXPERT), D_MODEL ** -0.5)
    moe_w_down = nrm((DEPTH, N_EXPERTS, D_EXPERT, D_MODEL), D_EXPERT ** -0.5)
    return {"x": x, "c": c, "ctx": ctx, "c_ctx": c_ctx, "ada_w": ada_w, "ada_b": ada_b,
            "norm1_g": norm1_g, "norm2_g": norm2_g,
            "ssd_w_in": ssd_w_in, "ssd_conv_w": ssd_conv_w, "ssd_conv_b": ssd_conv_b,
            "ssd_dt_bias": ssd_dt_bias, "ssd_a_log": ssd_a_log, "ssd_d": ssd_d,
            "ssd_norm_g": ssd_norm_g, "ssd_w_out": ssd_w_out,
            "da_w_in": da_w_in, "da_q_norm": da_q_norm, "da_k_norm": da_k_norm,
            "da_lam_q": da_lam_q, "da_lam_k": da_lam_k, "da_sub_norm": da_sub_norm, "da_w_out": da_w_out,
            "router_w": router_w, "router_b": router_b,
            "moe_w_gate": moe_w_gate, "moe_w_up": moe_w_up, "moe_w_down": moe_w_down}


def reference(x, c, ctx, c_ctx, ada_w, ada_b, norm1_g, norm2_g,
              ssd_w_in, ssd_conv_w, ssd_conv_b, ssd_dt_bias, ssd_a_log, ssd_d, ssd_norm_g, ssd_w_out,
              da_w_in, da_q_norm, da_k_norm, da_lam_q, da_lam_k, da_sub_norm, da_w_out,
              router_w, router_b, moe_w_gate, moe_w_up, moe_w_down):
    n_ctx = ctx.shape[1]
    xl, xc = x, ctx
    for i in range(DEPTH):
        need_ctx = i < DEPTH - 1
        mod_l = (jax.nn.silu(c) @ ada_w[i] + ada_b[i])[:, None, :]
        mod_c = (jax.nn.silu(c_ctx) @ ada_w[i] + ada_b[i])[None, None, :]
        sh1, sc1, g1, sh2, sc2, g2 = jnp.split(mod_l, 6, axis=-1)
        csh1, csc1, cg1, csh2, csc2, cg2 = jnp.split(mod_c, 6, axis=-1)
        hl = modulate(rmsnorm(xl, norm1_g[i]), sh1, sc1)
        hc = modulate(rmsnorm(xc, norm1_g[i]), csh1, csc1)
        j = i // N_MIXERS
        if i % N_MIXERS == 0:
            yl, yc = ssd_mixer(hl, hc, ssd_w_in[j], ssd_conv_w[j], ssd_conv_b[j], ssd_dt_bias[j],
                               ssd_a_log[j], ssd_d[j], ssd_norm_g[j], ssd_w_out[j], need_ctx)
        else:
            lam_init = 0.8 - 0.6 * math.exp(-0.3 * i)
            yl, yc = diff_attn_mixer(hl, hc, da_w_in[j], da_q_norm[j], da_k_norm[j], da_lam_q[j],
                                     da_lam_k[j], da_sub_norm[j], da_w_out[j], lam_init, need_ctx)
        xl = xl + g1 * yl
        hl2 = modulate(rmsnorm(xl, norm2_g[i]), sh2, sc2)
        if need_ctx:
            xc = xc + cg1 * yc
            hc2 = modulate(rmsnorm(xc, norm2_g[i]), csh2, csc2)
            m = grouped_moe(jnp.concatenate([hc2, hl2], axis=1), router_w, router_b,
                            moe_w_gate[i], moe_w_up[i], moe_w_down[i])
            xc = xc + cg2 * m[:, :n_ctx]
            xl = xl + g2 * m[:, n_ctx:]
        else:
            xl = xl + g2 * grouped_moe(hl2, router_w, router_b, moe_w_gate[i], moe_w_up[i], moe_w_down[i])
    return xl
```

```python
import functools
import math

import jax
import jax.numpy as jnp
from jax import lax
from jax.experimental import pallas as pl
from jax.experimental.pallas import tpu as pltpu

F32 = jnp.float32
BF16 = jnp.bfloat16

D_MODEL = 1024
DEPTH = 4
GRID_W = 64
RMS_EPS = 1e-6

SSD_INNER = 2048
SSD_HEADDIM = 64
SSD_HEADS = 32
SSD_GROUPS = 4
SSD_STATE = 128
SSD_CONV = 5
SSD_CHUNK = 128
SSD_BC = SSD_GROUPS * SSD_STATE
SSD_CONV_DIM = SSD_INNER + 2 * SSD_BC

DA_HEADS = 8
DA_HEAD_DIM = 64
DA_V_DIM = 128
ROPE_BASE = 10000.0

N_EXPERTS = 16
N_GROUPS = 4
EXPERTS_PER_GROUP = 4
D_EXPERT = 512
PAIRS_PER_GROUP = 6
N_COMBOS = N_GROUPS * PAIRS_PER_GROUP
PAIR_LO = (0, 0, 0, 1, 1, 2)
PAIR_HI = (1, 2, 3, 2, 3, 3)

LANES = 128
MOD_ROWS = 16
VMEM_LIMIT = 56 * 1024 * 1024

LOG2E = 1.4426950408889634


def _cparams(sem):
    return pltpu.CompilerParams(dimension_semantics=sem, vmem_limit_bytes=VMEM_LIMIT)


def _split_bf16(a, n):
    parts = []
    r = a
    for _ in range(n):
        p = r.astype(BF16)
        parts.append(p)
        r = r - p.astype(F32)
    return parts


def _dot_split(a, b_bf16, n, dims=(((1,), (0,)), ((), ()))):
    out = None
    for p in _split_bf16(a, n):
        t = lax.dot_general(p, b_bf16, dims, preferred_element_type=F32)
        out = t if out is None else out + t
    return out


_NT = (((1,), (1,)), ((), ()))
_TN = (((0,), (0,)), ((), ()))


def _sigmoid(x):
    return 1.0 / (1.0 + jnp.exp(-x))


def _silu(x):
    return x * _sigmoid(x)


def _rms_scale(x, n):
    return lax.rsqrt(jnp.sum(x * x, axis=-1, keepdims=True) * (1.0 / n) + RMS_EPS)


def _norm_mod(x, g, shift, scale):
    y = x * _rms_scale(x, x.shape[-1]) * g
    return y * (1.0 + scale) + shift


def _ada_kernel(c_ref, w_ref, b_ref, o_ref):
    s = _silu(c_ref[...])
    acc = jnp.dot(s, w_ref[0], precision=lax.Precision.HIGHEST, preferred_element_type=F32)
    o_ref[0] = acc + b_ref[0]


def _ada_mod(cvecs, ada_w, ada_b):
    depth, d, n6 = ada_w.shape
    tn = 1536
    return pl.pallas_call(
        _ada_kernel,
        grid=(depth, n6 // tn),
        in_specs=[pl.BlockSpec((MOD_ROWS, d), lambda i, j: (0, 0)),
                  pl.BlockSpec((1, d, tn), lambda i, j: (i, 0, j)),
                  pl.BlockSpec((1, 1, tn), lambda i, j: (i, 0, j))],
        out_specs=pl.BlockSpec((1, MOD_ROWS, tn), lambda i, j: (i, 0, j)),
        out_shape=jax.ShapeDtypeStruct((depth, MOD_ROWS, n6), F32),
        compiler_params=_cparams(("arbitrary", "arbitrary")),
        name="ada_mod",
    )(cvecs, ada_w, ada_b.reshape(depth, 1, n6))


class _Geom:
    def __init__(self, batch, n_ctx, n_seq):
        self.batch, self.n_ctx, self.n_seq = batch, n_ctx, n_seq
        self.nt = n_ctx + n_seq
        self.t = batch * self.nt
        self.tm = 256 if (n_ctx % 256 == 0 and n_seq % 256 == 0) else 128
        self.tiles_per_batch = self.nt // self.tm
        self.ctx_tiles = n_ctx // self.tm
        self.n_tiles = self.t // self.tm

    def mod_index(self, i):
        b = i // self.tiles_per_batch
        r = i % self.tiles_per_batch
        return 2 * b + (r >= self.ctx_tiles).astype(jnp.int32)


def _mod_spec(geom):
    return pl.BlockSpec((1, 1, 6 * D_MODEL), lambda i: (geom.mod_index(i), 0, 0))


def _row_spec(geom, width):
    return pl.BlockSpec((geom.tm, width), lambda i: (i, 0))


def _full_spec(shape):
    return pl.BlockSpec(shape, lambda i: (0,) * len(shape))


def _chunked_dot_store(h, w_ref, o_ref, chunk=512):
    n = w_ref.shape[1]
    for j in range(0, n, chunk):
        c = min(chunk, n - j)
        o_ref[:, j:j + c] = jnp.dot(h, w_ref[:, j:j + c],
                                    preferred_element_type=F32).astype(o_ref.dtype)


def _ssd_inproj_kernel(x_ref, mod_ref, g_ref, wz_ref, wx_ref, wdt_ref, z_ref, xbc_ref, dt_ref):
    d = D_MODEL
    mod = mod_ref[0]
    h = _norm_mod(x_ref[...], g_ref[...], mod[:, 0:d], mod[:, d:2 * d]).astype(BF16)
    _chunked_dot_store(h, wz_ref, z_ref)
    _chunked_dot_store(h, wx_ref, xbc_ref)
    dt_ref[...] = jnp.dot(h, wdt_ref[...], preferred_element_type=F32)


def _ssd_inproj(geom, x, modrows, g, wz, wx, wdt):
    t = geom.t
    return pl.pallas_call(
        _ssd_inproj_kernel,
        grid=(geom.n_tiles,),
        in_specs=[_row_spec(geom, D_MODEL), _mod_spec(geom), _full_spec((1, D_MODEL)),
                  _full_spec(wz.shape), _full_spec(wx.shape), _full_spec(wdt.shape)],
        out_specs=[_row_spec(geom, SSD_INNER), _row_spec(geom, SSD_CONV_DIM),
                   _row_spec(geom, 2 * SSD_HEADS)],
        out_shape=[jax.ShapeDtypeStruct((t, SSD_INNER), BF16),
                   jax.ShapeDtypeStruct((t, SSD_CONV_DIM), BF16),
                   jax.ShapeDtypeStruct((t, 2 * SSD_HEADS), F32)],
        compiler_params=_cparams(("arbitrary",)),
        name="ssd_inproj",
    )(x, modrows, g, wz, wx, wdt)


_CONV_PAD = 8


def _ssd_conv_kernel(x_ref, w_ref, b_ref, o_ref, pad_ref, *, segments, rows):
    half = SSD_CONV // 2
    w = w_ref[...]
    bias = b_ref[...]
    cols = x_ref.shape[2]
    zeros = jnp.zeros((_CONV_PAD, cols), F32)
    for start, length in segments:
        pad_ref[0:_CONV_PAD, :] = zeros
        pad_ref[_CONV_PAD + length:2 * _CONV_PAD + length, :] = zeros
        for r in range(0, length, rows):
            pad_ref[_CONV_PAD + r:_CONV_PAD + r + rows, :] = (
                x_ref[0, start + r:start + r + rows, :].astype(F32))
        for r in range(0, length, rows):
            acc = None
            for k in range(SSD_CONV):
                lo = _CONV_PAD + r + k - half
                term = pad_ref[lo:lo + rows, :] * w[k:k + 1, :]
                acc = term if acc is None else acc + term
            o_ref[0, start + r:start + r + rows, :] = _silu(acc + bias).astype(o_ref.dtype)


def _ssd_conv(geom, xbc, conv_w, conv_b):
    cols = 512
    segments = ((0, geom.n_ctx), (geom.n_ctx, geom.n_seq))
    kern = functools.partial(_ssd_conv_kernel, segments=segments, rows=geom.tm)
    xbc3 = xbc.reshape(geom.batch, geom.nt, SSD_CONV_DIM)
    out = pl.pallas_call(
        kern,
        grid=(geom.batch, SSD_CONV_DIM // cols),
        in_specs=[pl.BlockSpec((1, geom.nt, cols), lambda b, j: (b, 0, j)),
                  pl.BlockSpec((SSD_CONV, cols), lambda b, j: (0, j)),
                  pl.BlockSpec((1, cols), lambda b, j: (0, j))],
        out_specs=pl.BlockSpec((1, geom.nt, cols), lambda b, j: (b, 0, j)),
        out_shape=jax.ShapeDtypeStruct(xbc3.shape, BF16),
        scratch_shapes=[pltpu.VMEM((max(geom.n_ctx, geom.n_seq) + 2 * _CONV_PAD, cols), F32)],
        compiler_params=_cparams(("arbitrary", "arbitrary")),
        name="ssd_conv",
    )(xbc3, conv_w, conv_b.reshape(1, SSD_CONV_DIM))
    return out


def _ssd_scan_kernel(xbc_ref, dt_ref, bias_ref, alog_ref, expand_ref, o_ref, state_ref):
    k = pl.program_id(1)
    j = pl.program_id(2)
    L, H, P, G, N = SSD_CHUNK, SSD_HEADS, SSD_HEADDIM, SSD_GROUPS, SSD_STATE
    hg = H // G

    @pl.when(j == 0)
    def _():
        state_ref[...] = jnp.zeros_like(state_ref)

    fwd = k == 0
    dt_all = dt_ref[0]
    dt_raw = jnp.where(fwd, dt_all[:, :H], dt_all[:, H:])
    bias = jnp.where(fwd, bias_ref[0:1, :], bias_ref[1:2, :])
    alog = jnp.where(fwd, alog_ref[0:1, :], alog_ref[1:2, :])
    v = dt_raw + bias
    dtk = jnp.maximum(v, 0.0) + jnp.log(1.0 + jnp.exp(-jnp.abs(v)))
    a = -jnp.exp(alog) * dtk

    row = lax.broadcasted_iota(jnp.int32, (L, L), 0)
    col = lax.broadcasted_iota(jnp.int32, (L, L), 1)
    ahead = jnp.where(fwd, row - col, col - row)
    incl = ahead >= 0
    tri = incl.astype(BF16)
    tri_t = (ahead <= 0).astype(BF16)
    a_cs = None
    a_cs_t = None
    for p in _split_bf16(a, 3):
        t1 = jnp.dot(tri, p, preferred_element_type=F32)
        t2 = lax.dot_general(p, tri_t, _TN, preferred_element_type=F32)
        a_cs = t1 if a_cs is None else a_cs + t1
        a_cs_t = t2 if a_cs_t is None else a_cs_t + t2
    total = jnp.sum(a, axis=0, keepdims=True)

    expand = expand_ref[...]
    dt_x = _dot_split(dtk, expand, 3)
    acs_x = _dot_split(a_cs, expand, 3)
    tot_x = _dot_split(total, expand, 3)

    xs = xbc_ref[0, :, 0:SSD_INNER].astype(F32)
    bm = xbc_ref[0, :, SSD_INNER:SSD_INNER + SSD_BC]
    cm = xbc_ref[0, :, SSD_INNER + SSD_BC:SSD_INNER + 2 * SSD_BC]
    xdt = xs * dt_x
    xdt_b = xdt.astype(BF16)
    z_b = (xdt * jnp.exp(tot_x - acs_x)).astype(BF16)
    decay_in = jnp.exp(acs_x)
    state = state_ref[...]
    state_b = state.astype(BF16)
    lane = lax.broadcasted_iota(jnp.int32, (L, 2 * P), 1)
    first_head = lane < P

    new_state = []
    for g in range(G):
        bg = bm[:, g * N:(g + 1) * N]
        cg = cm[:, g * N:(g + 1) * N]
        cb = lax.dot_general(cg, bg, _NT, preferred_element_type=F32)
        cols = slice(g * hg * P, (g + 1) * hg * P)
        y_off = jnp.dot(cg, state_b[:, cols], preferred_element_type=F32) * decay_in[:, cols]
        for hp in range(hg // 2):
            h0 = g * hg + 2 * hp
            gs = []
            for h in (h0, h0 + 1):
                seg = jnp.exp(jnp.where(incl, a_cs[:, h:h + 1] - a_cs_t[h:h + 1, :], -jnp.inf))
                gs.append((cb * seg).astype(BF16))
            lhs = jnp.concatenate(gs, axis=1)
            xp = xdt_b[:, h0 * P:(h0 + 2) * P]
            zero = jnp.zeros_like(xp)
            rhs = jnp.concatenate([jnp.where(first_head, xp, zero),
                                   jnp.where(first_head, zero, xp)], axis=0)
            y_pair = jnp.dot(lhs, rhs, preferred_element_type=F32)
            lo = 2 * hp * P
            o_ref[0, 0, :, h0 * P:(h0 + 2) * P] = y_pair + y_off[:, lo:lo + 2 * P]
        upd = lax.dot_general(bg, z_b[:, cols], _TN, preferred_element_type=F32)
        new_state.append(state[:, cols] * jnp.exp(tot_x[:, cols]) + upd)
    for g in range(G):
        state_ref[:, g * hg * P:(g + 1) * hg * P] = new_state[g]


def _ssd_scan(geom, xbcc, dt, dt_bias, a_log):
    nch = geom.nt // SSD_CHUNK
    cch = geom.n_ctx // SSD_CHUNK

    def chunk_index(k, j):
        rev = jnp.where(j < cch, cch - 1 - j, nch - 1 - (j - cch))
        return jnp.where(k == 0, j, rev)

    expand = (jnp.arange(SSD_HEADS)[:, None] == (jnp.arange(SSD_INNER)[None, :] // SSD_HEADDIM)).astype(BF16)
    dt3 = dt.reshape(geom.batch, geom.nt, 2 * SSD_HEADS)
    return pl.pallas_call(
        _ssd_scan_kernel,
        grid=(geom.batch, 2, nch),
        in_specs=[pl.BlockSpec((1, SSD_CHUNK, SSD_CONV_DIM), lambda b, k, j: (b, chunk_index(k, j), 0)),
                  pl.BlockSpec((1, SSD_CHUNK, 2 * SSD_HEADS), lambda b, k, j: (b, chunk_index(k, j), 0)),
                  pl.BlockSpec((2, SSD_HEADS), lambda b, k, j: (0, 0)),
                  pl.BlockSpec((2, SSD_HEADS), lambda b, k, j: (0, 0)),
                  pl.BlockSpec((SSD_HEADS, SSD_INNER), lambda b, k, j: (0, 0))],
        out_specs=pl.BlockSpec((1, 1, SSD_CHUNK, SSD_INNER), lambda b, k, j: (k, b, chunk_index(k, j), 0)),
        out_shape=jax.ShapeDtypeStruct((2, geom.batch, geom.nt, SSD_INNER), F32),
        scratch_shapes=[pltpu.VMEM((SSD_STATE, SSD_INNER), F32)],
        compiler_params=_cparams(("arbitrary", "arbitrary", "arbitrary")),
        name="ssd_scan",
    )(xbcc, dt3, dt_bias, a_log, expand)


def _route(h2, rwhi_ref, rwlo_ref, rb_ref, rt_ref):
    tm = h2.shape[0]
    h_hi, h_lo = _split_bf16(h2, 2)
    rw_hi, rw_lo = rwhi_ref[...], rwlo_ref[...]
    logits = (lax.dot_general(rw_hi, h_hi, _NT, preferred_element_type=F32)
              + lax.dot_general(rw_hi, h_lo, _NT, preferred_element_type=F32)
              + lax.dot_general(rw_lo, h_hi, _NT, preferred_element_type=F32))
    scores = _sigmoid(logits)
    biased = scores + rb_ref[...]
    s = [scores[e:e + 1, :] for e in range(N_EXPERTS)]
    b = [biased[e:e + 1, :] for e in range(N_EXPERTS)]

    def pair_max(v):
        m = v[0] + v[1]
        for lo, hi in zip(PAIR_LO[1:], PAIR_HI[1:]):
            m = jnp.maximum(m, v[lo] + v[hi])
        return m

    gsc = [pair_max(b[4 * g:4 * g + 4]) for g in range(N_GROUPS)]
    gbest = jnp.zeros((1, tm), jnp.int32)
    best = gsc[0]
    for g in range(1, N_GROUPS):
        take = gsc[g] > best
        gbest = jnp.where(take, g, gbest)
        best = jnp.where(take, gsc[g], best)

    def pick(vals, j):
        out = vals[j]
        for g in range(1, N_GROUPS):
            out = jnp.where(gbest == g, vals[4 * g + j], out)
        return out

    vb = [pick(b, j) for j in range(EXPERTS_PER_GROUP)]
    vs = [pick(s, j) for j in range(EXPERTS_PER_GROUP)]

    def argmax4(v):
        idx = jnp.zeros((1, tm), jnp.int32)
        m = v[0]
        for j in range(1, EXPERTS_PER_GROUP):
            take = v[j] > m
            idx = jnp.where(take, j, idx)
            m = jnp.where(take, v[j], m)
        return idx

    i1 = argmax4(vb)
    i2 = argmax4([jnp.where(i1 == j, -jnp.inf, vb[j]) for j in range(EXPERTS_PER_GROUP)])
    s1 = sum(jnp.where(i1 == j, vs[j], 0.0) for j in range(EXPERTS_PER_GROUP))
    s2 = sum(jnp.where(i2 == j, vs[j], 0.0) for j in range(EXPERTS_PER_GROUP))
    w1 = s1 / (s1 + s2)
    w2 = s2 / (s1 + s2)
    lo = jnp.minimum(i1, i2)
    hi = jnp.maximum(i1, i2)
    pair = jnp.where(lo == 0, hi - 1, jnp.where(lo == 1, hi + 1, 5))
    combo = gbest * PAIRS_PER_GROUP + pair
    first_lo = i1 < i2
    rt_ref[0:1, :] = combo.astype(F32)
    rt_ref[1:2, :] = jnp.where(first_lo, w1, w2)
    rt_ref[2:3, :] = jnp.where(first_lo, w2, w1)
    rt_ref[3:8, :] = jnp.zeros((5, tm), F32)


def _residual_route(x, y, mod, n2g, rwhi_ref, rwlo_ref, rb_ref, xo_ref, h2_ref, rt_ref):
    d = D_MODEL
    x_new = x + mod[:, 2 * d:3 * d] * y
    xo_ref[...] = x_new
    h2 = _norm_mod(x_new, n2g, mod[:, 3 * d:4 * d], mod[:, 4 * d:5 * d])
    h2_ref[...] = h2
    _route(h2, rwhi_ref, rwlo_ref, rb_ref, rt_ref)


def _ssd_out_kernel(yd_ref, xs_ref, z_ref, x_ref, mod_ref, dsk_ref, ng_ref, w_ref, n2g_ref,
                    rwhi_ref, rwlo_ref, rb_ref, xo_ref, h2_ref, rt_ref):
    y = dsk_ref[...] * xs_ref[...].astype(F32) + yd_ref[0] + yd_ref[1]
    y = y * _silu(z_ref[...].astype(F32))
    yn = (y * _rms_scale(y, SSD_INNER) * ng_ref[...]).astype(BF16)
    out = jnp.dot(yn, w_ref[...], preferred_element_type=F32)
    _residual_route(x_ref[...], out, mod_ref[0], n2g_ref[...], rwhi_ref, rwlo_ref, rb_ref,
                    xo_ref, h2_ref, rt_ref)


def _route_specs(geom):
    in_specs = [_full_spec((1, D_MODEL)), _full_spec((N_EXPERTS, D_MODEL)),
                _full_spec((N_EXPERTS, D_MODEL)), _full_spec((N_EXPERTS, 1))]
    out_specs = [_row_spec(geom, D_MODEL), _row_spec(geom, D_MODEL),
                 pl.BlockSpec((8, geom.tm), lambda i: (i, 0))]
    out_shape = [jax.ShapeDtypeStruct((geom.t, D_MODEL), F32),
                 jax.ShapeDtypeStruct((geom.t, D_MODEL), F32),
                 jax.ShapeDtypeStruct((geom.n_tiles * 8, geom.tm), F32)]
    return in_specs, out_specs, out_shape


def _ssd_out(geom, ydir, xbcc, z, x, modrows, dskip_x, norm_g, w_out, n2g, rw_hi, rw_lo, rb):
    r_in, r_out, r_shape = _route_specs(geom)
    tm = geom.tm
    yd2 = ydir.reshape(2, geom.t, SSD_INNER)
    xbcc2 = xbcc.reshape(geom.t, SSD_CONV_DIM)
    return pl.pallas_call(
        _ssd_out_kernel,
        grid=(geom.n_tiles,),
        in_specs=[pl.BlockSpec((2, tm, SSD_INNER), lambda i: (0, i, 0)),
                  pl.BlockSpec((tm, SSD_INNER), lambda i: (i, 0)),
                  _row_spec(geom, SSD_INNER), _row_spec(geom, D_MODEL), _mod_spec(geom),
                  _full_spec((1, SSD_INNER)), _full_spec((1, SSD_INNER)),
                  _full_spec(w_out.shape)] + r_in,
        out_specs=r_out,
        out_shape=r_shape,
        compiler_params=_cparams(("arbitrary",)),
        name="ssd_out",
    )(yd2, xbcc2, z, x, modrows, dskip_x, norm_g, w_out, n2g, rw_hi, rw_lo, rb)


def _da_inproj_kernel(x_ref, mod_ref, g_ref, wq_ref, wk_ref, wv_ref, gq_ref, gk_ref,
                      cos_ref, sin_ref, gsum_ref, gexp_ref, q_ref, k_ref, v_ref):
    d = D_MODEL
    mod = mod_ref[0]
    h = _norm_mod(x_ref[...], g_ref[...], mod[:, 0:d], mod[:, d:2 * d]).astype(BF16)
    _chunked_dot_store(h, wv_ref, v_ref)
    cos = cos_ref[...]
    sin = sin_ref[...]
    lane = lax.broadcasted_iota(jnp.int32, cos.shape, 1)
    first_half = (lane % 32) < 16
    for w_ref, gn_ref, o_ref in ((wq_ref, gq_ref, q_ref), (wk_ref, gk_ref, k_ref)):
        y = jnp.dot(h, w_ref[...], preferred_element_type=F32)
        ss = jnp.dot((y * y).astype(BF16), gsum_ref[...], preferred_element_type=F32)
        r = lax.rsqrt(ss * (1.0 / DA_HEAD_DIM) + RMS_EPS)
        rx = _dot_split(r, gexp_ref[...], 2)
        yn = y * rx * gn_ref[...]
        for hd in range(DA_HEADS):
            c = yn[:, hd * LANES:(hd + 1) * LANES]
            partner = jnp.where(first_half, pltpu.roll(c, LANES - 16, 1), pltpu.roll(c, 16, 1))
            o_ref[:, hd * LANES:(hd + 1) * LANES] = (c * cos + partner * sin).astype(o_ref.dtype)


def _da_inproj(geom, x, modrows, g, wq, wk, wv, gq, gk, cos_t, sin_t, gsum, gexp):
    t = geom.t
    tpb = geom.tiles_per_batch
    tab_spec = pl.BlockSpec((geom.tm, LANES), lambda i: (i % tpb, 0))
    return pl.pallas_call(
        _da_inproj_kernel,
        grid=(geom.n_tiles,),
        in_specs=[_row_spec(geom, D_MODEL), _mod_spec(geom), _full_spec((1, D_MODEL)),
                  _full_spec(wq.shape), _full_spec(wk.shape), _full_spec(wv.shape),
                  _full_spec((1, D_MODEL)), _full_spec((1, D_MODEL)), tab_spec, tab_spec,
                  _full_spec(gsum.shape), _full_spec(gexp.shape)],
        out_specs=[_row_spec(geom, D_MODEL)] * 3,
        out_shape=[jax.ShapeDtypeStruct((t, D_MODEL), BF16)] * 3,
        compiler_params=_cparams(("arbitrary",)),
        name="da_inproj",
    )(x, modrows, g, wq, wk, wv, gq, gk, cos_t, sin_t, gsum, gexp)


def _da_attn_kernel(q_ref, k_ref, v_ref, lq_ref, lk_ref, sg_ref, o_ref, *, tk, ctx_qtiles,
                    ctx_chunks, all_chunks, lam_init):
    qi = pl.program_id(2)
    q = q_ref[0]
    tq = q.shape[0]
    lane = lax.broadcasted_iota(jnp.int32, q.shape, 1)
    zero = jnp.zeros_like(q)
    q1 = jnp.where(lane < DA_HEAD_DIM, q, zero)
    q2 = jnp.where(lane < DA_HEAD_DIM, zero, q)
    n_chunks = jnp.where(qi < ctx_qtiles, ctx_chunks, all_chunks)

    def softmax_step(s, m, l, acc, vc):
        m_new = jnp.maximum(m, jnp.max(s, axis=-1, keepdims=True))
        alpha = jnp.exp2(m - m_new)
        p = jnp.exp2(s - m_new)
        l_new = alpha * l + jnp.sum(p, axis=-1, keepdims=True)
        acc_new = alpha * acc + jnp.dot(p.astype(BF16), vc, preferred_element_type=F32)
        return m_new, l_new, acc_new

    def body(j, carry):
        m1, l1, a1, m2, l2, a2 = carry
        off = pl.multiple_of(j * tk, tk)
        kc = k_ref[0, pl.ds(off, tk), :]
        vc = v_ref[0, pl.ds(off, tk), :]
        s1 = lax.dot_general(q1, kc, _NT, preferred_element_type=F32)
        s2 = lax.dot_general(q2, kc, _NT, preferred_element_type=F32)
        m1, l1, a1 = softmax_step(s1, m1, l1, a1, vc)
        m2, l2, a2 = softmax_step(s2, m2, l2, a2, vc)
        return m1, l1, a1, m2, l2, a2

    neg = jnp.full((tq, 1), -1e30, F32)
    zl = jnp.zeros((tq, 1), F32)
    za = jnp.zeros((tq, DA_V_DIM), F32)
    m1, l1, a1, m2, l2, a2 = lax.fori_loop(0, n_chunks, body, (neg, zl, za, neg, zl, za))

    tdot = jnp.sum(lq_ref[...] * lk_ref[...], axis=-1, keepdims=True)
    e = jnp.exp(tdot)
    lam = e[0:1, :] - e[1:2, :] + lam_init
    o = a1 / l1 - lam * (a2 / l2)
    o = o * _rms_scale(o, DA_V_DIM) * sg_ref[...] * (1.0 - lam_init)
    o_ref[0] = o.astype(o_ref.dtype)


def _da_attn(geom, q, k, v, lam_q, lam_k, sub_g, lam_init):
    tq = geom.tm
    tk = 256 if geom.n_ctx % 256 == 0 else 128
    b, nt = geom.batch, geom.nt
    q3, k3, v3 = (a.reshape(b, nt, D_MODEL) for a in (q, k, v))
    kern = functools.partial(_da_attn_kernel, tk=tk, ctx_qtiles=geom.n_ctx // tq,
                             ctx_chunks=geom.n_ctx // tk, all_chunks=nt // tk, lam_init=lam_init)
    kv_spec = pl.BlockSpec((1, nt, LANES), lambda bi, h, i: (bi, 0, h))
    q_spec = pl.BlockSpec((1, tq, LANES), lambda bi, h, i: (bi, i, h))
    small = lambda shape: pl.BlockSpec(shape, lambda bi, h, i: (0, 0))
    return pl.pallas_call(
        kern,
        grid=(b, DA_HEADS, nt // tq),
        in_specs=[q_spec, kv_spec, kv_spec, small((2, DA_HEAD_DIM)), small((2, DA_HEAD_DIM)),
                  small((1, DA_V_DIM))],
        out_specs=q_spec,
        out_shape=jax.ShapeDtypeStruct((b, nt, D_MODEL), BF16),
        compiler_params=_cparams(("arbitrary", "arbitrary", "arbitrary")),
        name="da_attn",
    )(q3, k3, v3, lam_q, lam_k, sub_g)


def _da_out_kernel(o_ref, x_ref, mod_ref, w_ref, n2g_ref, rwhi_ref, rwlo_ref, rb_ref,
                   xo_ref, h2_ref, rt_ref):
    out = jnp.dot(o_ref[...], w_ref[...], preferred_element_type=F32)
    _residual_route(x_ref[...], out, mod_ref[0], n2g_ref[...], rwhi_ref, rwlo_ref, rb_ref,
                    xo_ref, h2_ref, rt_ref)


def _da_out(geom, o, x, modrows, w_out, n2g, rw_hi, rw_lo, rb):
    r_in, r_out, r_shape = _route_specs(geom)
    return pl.pallas_call(
        _da_out_kernel,
        grid=(geom.n_tiles,),
        in_specs=[_row_spec(geom, D_MODEL), _row_spec(geom, D_MODEL), _mod_spec(geom),
                  _full_spec(w_out.shape)] + r_in,
        out_specs=r_out,
        out_shape=r_shape,
        compiler_params=_cparams(("arbitrary",)),
        name="da_out",
    )(o.reshape(geom.t, D_MODEL), x, modrows, w_out, n2g, rw_hi, rw_lo, rb)


def _gather_rows(idx_ref, base, src_hbm, dst, sem, rows):
    def body(r, carry):
        t = idx_ref[base + r]
        pltpu.make_async_copy(src_hbm.at[pl.ds(t, 1), :], dst.at[pl.ds(r, 1), :], sem).start()
        return carry
    lax.fori_loop(0, rows, body, 0)


def _wait_rows(src_hbm, dst, sem, rows):
    pltpu.make_async_copy(src_hbm.at[pl.ds(0, rows), :], dst, sem).wait()


def _moe_kernel(e0_ref, e1_ref, used_ref, tok_ref, h2_hbm, wt_ref,
                wg0_ref, wu0_ref, wd0_ref, wg1_ref, wu1_ref, wd1_ref, y_ref, xbuf, sem):
    del e0_ref, e1_ref
    i = pl.program_id(0)
    tm = y_ref.shape[0]
    n_used = used_ref[0]
    slot = i % 2

    @pl.when(jnp.logical_and(i == 0, n_used > 0))
    def _():
        _gather_rows(tok_ref, 0, h2_hbm, xbuf.at[0], sem.at[0], tm)

    @pl.when(i + 1 < n_used)
    def _():
        _gather_rows(tok_ref, (i + 1) * tm, h2_hbm, xbuf.at[1 - slot], sem.at[1 - slot], tm)

    @pl.when(i < n_used)
    def _():
        _wait_rows(h2_hbm, xbuf.at[slot], sem.at[slot], tm)
        x = xbuf[slot].astype(BF16)
        wt = wt_ref[...]
        y = None
        for e, (wg, wu, wd) in enumerate(((wg0_ref, wu0_ref, wd0_ref), (wg1_ref, wu1_ref, wd1_ref))):
            gate = jnp.dot(x, wg[0], preferred_element_type=F32)
            up = jnp.dot(x, wu[0], preferred_element_type=F32)
            hid = (_silu(gate) * up).astype(BF16)
            ye = jnp.dot(hid, wd[0], preferred_element_type=F32) * wt[:, e:e + 1]
            y = ye if y is None else y + ye
        y_ref[...] = y

    @pl.when(i >= n_used)
    def _():
        y_ref[...] = jnp.zeros_like(y_ref)


def _moe_plan(geom, route):
    tm, t = geom.tm, geom.t
    rt = route.reshape(geom.n_tiles, 8, tm)
    combo = rt[:, 0, :].reshape(t).astype(jnp.int32)
    w_lo = rt[:, 1, :].reshape(t)
    w_hi = rt[:, 2, :].reshape(t)
    onehot = (combo[:, None] == jnp.arange(N_COMBOS, dtype=jnp.int32)[None, :]).astype(jnp.int32)
    csum = jnp.cumsum(onehot, axis=0)
    counts = csum[-1]
    rank = jnp.sum(csum * onehot, axis=1) - 1
    padded = ((counts + tm - 1) // tm) * tm
    ends = jnp.cumsum(padded)
    offs = ends - padded
    pos = offs[combo] + rank
    p_rows = t + N_COMBOS * tm
    n_ptiles = p_rows // tm
    tok_sorted = jnp.zeros((p_rows,), jnp.int32).at[pos].set(jnp.arange(t, dtype=jnp.int32))
    wt_sorted = jnp.zeros((p_rows, 2), F32).at[pos].set(jnp.stack([w_lo, w_hi], axis=1))
    n_used = (ends[-1] // tm).astype(jnp.int32)
    tile_start = jnp.arange(n_ptiles, dtype=jnp.int32) * tm
    last_start = jnp.maximum(n_used - 1, 0) * tm
    tile_combo = jnp.searchsorted(ends, jnp.minimum(tile_start, last_start), side="right")
    tile_combo = jnp.minimum(tile_combo, N_COMBOS - 1).astype(jnp.int32)
    grp = tile_combo // PAIRS_PER_GROUP
    pair = tile_combo % PAIRS_PER_GROUP
    e0 = grp * EXPERTS_PER_GROUP + jnp.asarray(PAIR_LO, jnp.int32)[pair]
    e1 = grp * EXPERTS_PER_GROUP + jnp.asarray(PAIR_HI, jnp.int32)[pair]
    return e0, e1, n_used.reshape(1), tok_sorted, wt_sorted, pos


def _moe_experts(geom, h2, plan, wg, wu, wd):
    e0, e1, n_used, tok_sorted, wt_sorted, _ = plan
    tm = geom.tm
    p_rows = tok_sorted.shape[0]
    de = wg.shape[2]

    def wspec(shape, which):
        if which == 0:
            return pl.BlockSpec(shape, lambda i, e0r, e1r, ur, tr: (e0r[i], 0, 0))
        return pl.BlockSpec(shape, lambda i, e0r, e1r, ur, tr: (e1r[i], 0, 0))

    grid_spec = pltpu.PrefetchScalarGridSpec(
        num_scalar_prefetch=4,
        grid=(p_rows // tm,),
        in_specs=[pl.BlockSpec(memory_space=pl.ANY),
                  pl.BlockSpec((tm, 2), lambda i, *_: (i, 0)),
                  wspec((1, D_MODEL, de), 0), wspec((1, D_MODEL, de), 0), wspec((1, de, D_MODEL), 0),
                  wspec((1, D_MODEL, de), 1), wspec((1, D_MODEL, de), 1), wspec((1, de, D_MODEL), 1)],
        out_specs=pl.BlockSpec((tm, D_MODEL), lambda i, *_: (i, 0)),
        scratch_shapes=[pltpu.VMEM((2, tm, D_MODEL), F32), pltpu.SemaphoreType.DMA((2,))],
    )
    return pl.pallas_call(
        _moe_kernel,
        grid_spec=grid_spec,
        out_shape=jax.ShapeDtypeStruct((p_rows, D_MODEL), F32),
        compiler_params=_cparams(("arbitrary",)),
        name="moe_experts",
    )(e0, e1, n_used, tok_sorted, h2, wt_sorted, wg, wu, wd, wg, wu, wd)


def _moe_combine_kernel(pos_ref, x_ref, mod_ref, y_hbm, o_ref, ybuf, sem):
    i = pl.program_id(0)
    n = pl.num_programs(0)
    tm = o_ref.shape[0]
    slot = i % 2
    d = D_MODEL

    @pl.when(i == 0)
    def _():
        _gather_rows(pos_ref, 0, y_hbm, ybuf.at[0], sem.at[0], tm)

    @pl.when(i + 1 < n)
    def _():
        _gather_rows(pos_ref, (i + 1) * tm, y_hbm, ybuf.at[1 - slot], sem.at[1 - slot], tm)

    _wait_rows(y_hbm, ybuf.at[slot], sem.at[slot], tm)
    mod = mod_ref[0]
    o_ref[...] = x_ref[...] + mod[:, 5 * d:6 * d] * ybuf[slot]


def _moe_combine(geom, x, modrows, y_sorted, pos):
    tm = geom.tm
    grid_spec = pltpu.PrefetchScalarGridSpec(
        num_scalar_prefetch=1,
        grid=(geom.n_tiles,),
        in_specs=[pl.BlockSpec((tm, D_MODEL), lambda i, p: (i, 0)),
                  pl.BlockSpec((1, 1, 6 * D_MODEL), lambda i, p: (geom.mod_index(i), 0, 0)),
                  pl.BlockSpec(memory_space=pl.ANY)],
        out_specs=pl.BlockSpec((tm, D_MODEL), lambda i, p: (i, 0)),
        scratch_shapes=[pltpu.VMEM((2, tm, D_MODEL), F32), pltpu.SemaphoreType.DMA((2,))],
    )
    return pl.pallas_call(
        _moe_combine_kernel,
        grid_spec=grid_spec,
        out_shape=jax.ShapeDtypeStruct((geom.t, D_MODEL), F32),
        compiler_params=_cparams(("arbitrary",)),
        name="moe_combine",
    )(pos, x, modrows, y_sorted)


def _rope_tables(geom):
    tpos = jnp.arange(geom.n_seq, dtype=jnp.int32)
    pos = jnp.stack([tpos // GRID_W, tpos % GRID_W], axis=-1).astype(F32)
    n_freq = DA_HEAD_DIM // 4
    inv_freq = ROPE_BASE ** (-jnp.arange(n_freq, dtype=F32) / n_freq)
    ang = pos[..., None] * inv_freq
    cos, sin = jnp.cos(ang), jnp.sin(ang)
    cos_l = jnp.broadcast_to(cos[:, None, :, None, :], (geom.n_seq, 2, 2, 2, n_freq))
    sgn = jnp.asarray([-1.0, 1.0], F32)[None, None, None, :, None]
    sin_l = jnp.broadcast_to(sin[:, None, :, None, :], (geom.n_seq, 2, 2, 2, n_freq)) * sgn
    cos_l = cos_l.reshape(geom.n_seq, LANES)
    sin_l = sin_l.reshape(geom.n_seq, LANES)
    cos_t = jnp.concatenate([jnp.ones((geom.n_ctx, LANES), F32), cos_l], axis=0)
    sin_t = jnp.concatenate([jnp.zeros((geom.n_ctx, LANES), F32), sin_l], axis=0)
    return cos_t, sin_t


def _head_major(w):
    return w.reshape(D_MODEL, 2, DA_HEADS, DA_HEAD_DIM).transpose(0, 2, 1, 3).reshape(D_MODEL, D_MODEL)


def kernel(x, c, ctx, c_ctx, ada_w, ada_b, norm1_g, norm2_g, ssd_w_in, ssd_conv_w, ssd_conv_b,
           ssd_dt_bias, ssd_a_log, ssd_d, ssd_norm_g, ssd_w_out, da_w_in, da_q_norm, da_k_norm,
           da_lam_q, da_lam_k, da_sub_norm, da_w_out, router_w, router_b, moe_w_gate, moe_w_up,
           moe_w_down):
    batch, n_seq, d = x.shape
    n_ctx = ctx.shape[1]
    depth = ada_w.shape[0]
    assert d == D_MODEL and batch + 1 <= MOD_ROWS
    geom = _Geom(batch, n_ctx, n_seq)

    cvecs = jnp.zeros((MOD_ROWS, d), F32).at[:batch].set(c).at[batch].set(c_ctx)
    mod_all = _ada_mod(cvecs, ada_w, ada_b)

    xs = jnp.concatenate([ctx, x], axis=1).reshape(geom.t, d)

    rw_t = router_w.T
    rw_hi = rw_t.astype(BF16)
    rw_lo = (rw_t - rw_hi.astype(F32)).astype(BF16)
    rb = router_b.reshape(N_EXPERTS, 1)

    cos_t, sin_t = _rope_tables(geom)
    grp_of_col = jnp.arange(D_MODEL) // DA_HEAD_DIM
    gsum = (grp_of_col[:, None] == jnp.arange(LANES)[None, :]).astype(BF16)
    gexp = gsum.T

    for i in range(depth):
        mod_i = mod_all[i]
        modrows = jnp.stack([jnp.broadcast_to(mod_i[batch], (batch, 6 * d)), mod_i[:batch]],
                            axis=1).reshape(2 * batch, 1, 6 * d)
        g1 = norm1_g[i].reshape(1, d)
        g2 = norm2_g[i].reshape(1, d)
        j = i // 2
        if i % 2 == 0:
            w_in = ssd_w_in[j].astype(BF16)
            wz = w_in[:, :SSD_INNER]
            wx = w_in[:, SSD_INNER:SSD_INNER + SSD_CONV_DIM]
            wdt = w_in[:, SSD_INNER + SSD_CONV_DIM:]
            z, xbc, dt = _ssd_inproj(geom, xs, modrows, g1, wz, wx, wdt)
            xbcc = _ssd_conv(geom, xbc, ssd_conv_w[j], ssd_conv_b[j])
            ydir = _ssd_scan(geom, xbcc, dt, ssd_dt_bias[j], ssd_a_log[j])
            dskip_x = jnp.repeat(ssd_d[j], SSD_HEADDIM).reshape(1, SSD_INNER)
            xs, h2, route = _ssd_out(geom, ydir, xbcc, z, xs, modrows, dskip_x,
                                     ssd_norm_g[j].reshape(1, SSD_INNER),
                                     ssd_w_out[j].astype(BF16), g2, rw_hi, rw_lo, rb)
        else:
            lam_init = 0.8 - 0.6 * math.exp(-0.3 * i)
            w_in = da_w_in[j]
            wq = _head_major(w_in[:, :D_MODEL]).astype(BF16)
            wk = _head_major(w_in[:, D_MODEL:2 * D_MODEL]).astype(BF16)
            wv = w_in[:, 2 * D_MODEL:].astype(BF16)
            q_scale = DA_HEAD_DIM ** -0.5 * LOG2E
            gq = jnp.tile(da_q_norm[j].reshape(2 * DA_HEAD_DIM), DA_HEADS).reshape(1, d) * q_scale
            gk = jnp.tile(da_k_norm[j].reshape(2 * DA_HEAD_DIM), DA_HEADS).reshape(1, d)
            q, k, v = _da_inproj(geom, xs, modrows, g1, wq, wk, wv, gq, gk, cos_t, sin_t, gsum, gexp)
            o = _da_attn(geom, q, k, v, da_lam_q[j], da_lam_k[j],
                         da_sub_norm[j].reshape(1, DA_V_DIM), lam_init)
            xs, h2, route = _da_out(geom, o, xs, modrows, da_w_out[j].astype(BF16), g2,
                                    rw_hi, rw_lo, rb)
        plan = _moe_plan(geom, route)
        y_sorted = _moe_experts(geom, h2, plan, moe_w_gate[i].astype(BF16),
                                moe_w_up[i].astype(BF16), moe_w_down[i].astype(BF16))
        xs = _moe_combine(geom, xs, modrows, y_sorted, plan[5])

    return xs.reshape(batch, geom.nt, d)[:, n_ctx:, :]
```

```python
import functools
import math

import jax
import jax.numpy as jnp
from jax import lax
from jax.experimental import pallas as pl
from jax.experimental.pallas import tpu as pltpu

F32 = jnp.float32
BF16 = jnp.bfloat16

D_MODEL = 1024
DEPTH = 4
GRID_W = 64
RMS_EPS = 1e-6

SSD_INNER = 2048
SSD_HEADDIM = 64
SSD_HEADS = 32
SSD_GROUPS = 4
SSD_STATE = 128
SSD_CONV = 5
SSD_CHUNK = 128
SSD_BC = SSD_GROUPS * SSD_STATE
SSD_CONV_DIM = SSD_INNER + 2 * SSD_BC

DA_HEADS = 8
DA_HEAD_DIM = 64
DA_V_DIM = 128
ROPE_BASE = 10000.0

N_EXPERTS = 16
N_GROUPS = 4
EXPERTS_PER_GROUP = 4
D_EXPERT = 512
PAIRS_PER_GROUP = 6
N_COMBOS = N_GROUPS * PAIRS_PER_GROUP
PAIR_LO = (0, 0, 0, 1, 1, 2)
PAIR_HI = (1, 2, 3, 2, 3, 3)

LANES = 128
MOD_ROWS = 16
VMEM_LIMIT = 56 * 1024 * 1024

LOG2E = 1.4426950408889634


def _cparams(sem):
    return pltpu.CompilerParams(dimension_semantics=sem, vmem_limit_bytes=VMEM_LIMIT)


def _split_bf16(a, n):
    parts = []
    r = a
    for _ in range(n):
        p = r.astype(BF16)
        parts.append(p)
        r = r - p.astype(F32)
    return parts


def _dot_split(a, b_bf16, n, dims=(((1,), (0,)), ((), ()))):
    out = None
    for p in _split_bf16(a, n):
        t = lax.dot_general(p, b_bf16, dims, preferred_element_type=F32)
        out = t if out is None else out + t
    return out


_NT = (((1,), (1,)), ((), ()))
_TN = (((0,), (0,)), ((), ()))


def _sigmoid(x):
    return 1.0 / (1.0 + jnp.exp(-x))


def _silu(x):
    return x * _sigmoid(x)


def _rms_scale(x, n):
    return lax.rsqrt(jnp.sum(x * x, axis=-1, keepdims=True) * (1.0 / n) + RMS_EPS)


def _norm_mod(x, g, shift, scale):
    y = x * _rms_scale(x, x.shape[-1]) * g
    return y * (1.0 + scale) + shift


def _ada_kernel(c_ref, w_ref, b_ref, o_ref):
    s = _silu(c_ref[...])
    acc = jnp.dot(s, w_ref[0], precision=lax.Precision.HIGHEST, preferred_element_type=F32)
    o_ref[0] = acc + b_ref[0]


def _ada_mod(cvecs, ada_w, ada_b):
    depth, d, n6 = ada_w.shape
    tn = 1536
    return pl.pallas_call(
        _ada_kernel,
        grid=(depth, n6 // tn),
        in_specs=[pl.BlockSpec((MOD_ROWS, d), lambda i, j: (0, 0)),
                  pl.BlockSpec((1, d, tn), lambda i, j: (i, 0, j)),
                  pl.BlockSpec((1, 1, tn), lambda i, j: (i, 0, j))],
        out_specs=pl.BlockSpec((1, MOD_ROWS, tn), lambda i, j: (i, 0, j)),
        out_shape=jax.ShapeDtypeStruct((depth, MOD_ROWS, n6), F32),
        compiler_params=_cparams(("arbitrary", "arbitrary")),
        name="ada_mod",
    )(cvecs, ada_w, ada_b.reshape(depth, 1, n6))


class _Geom:
    def __init__(self, batch, n_ctx, n_seq):
        self.batch, self.n_ctx, self.n_seq = batch, n_ctx, n_seq
        self.nt = n_ctx + n_seq
        self.t = batch * self.nt
        self.tm = 256 if (n_ctx % 256 == 0 and n_seq % 256 == 0) else 128
        self.tiles_per_batch = self.nt // self.tm
        self.ctx_tiles = n_ctx // self.tm
        self.n_tiles = self.t // self.tm

    def mod_index(self, i):
        b = i // self.tiles_per_batch
        r = i % self.tiles_per_batch
        return 2 * b + (r >= self.ctx_tiles).astype(jnp.int32)


def _mod_spec(geom):
    return pl.BlockSpec((1, 1, 6 * D_MODEL), lambda i: (geom.mod_index(i), 0, 0))


def _row_spec(geom, width):
    return pl.BlockSpec((geom.tm, width), lambda i: (i, 0))


def _full_spec(shape):
    return pl.BlockSpec(shape, lambda i: (0,) * len(shape))


def _chunked_dot_store(h, w_ref, o_ref, chunk=512):
    n = w_ref.shape[1]
    for j in range(0, n, chunk):
        c = min(chunk, n - j)
        o_ref[:, j:j + c] = jnp.dot(h, w_ref[:, j:j + c],
                                    preferred_element_type=F32).astype(o_ref.dtype)


def _ssd_inproj_kernel(x_ref, mod_ref, g_ref, wz_ref, wx_ref, wdt_ref, z_ref, xbc_ref, dt_ref):
    d = D_MODEL
    mod = mod_ref[0]
    h = _norm_mod(x_ref[...], g_ref[...], mod[:, 0:d], mod[:, d:2 * d]).astype(BF16)
    _chunked_dot_store(h, wz_ref, z_ref)
    _chunked_dot_store(h, wx_ref, xbc_ref)
    dt_ref[...] = jnp.dot(h, wdt_ref[...], preferred_element_type=F32)


def _ssd_inproj(geom, x, modrows, g, wz, wx, wdt):
    t = geom.t
    return pl.pallas_call(
        _ssd_inproj_kernel,
        grid=(geom.n_tiles,),
        in_specs=[_row_spec(geom, D_MODEL), _mod_spec(geom), _full_spec((1, D_MODEL)),
                  _full_spec(wz.shape), _full_spec(wx.shape), _full_spec(wdt.shape)],
        out_specs=[_row_spec(geom, SSD_INNER), _row_spec(geom, SSD_CONV_DIM),
                   _row_spec(geom, 2 * SSD_HEADS)],
        out_shape=[jax.ShapeDtypeStruct((t, SSD_INNER), BF16),
                   jax.ShapeDtypeStruct((t, SSD_CONV_DIM), BF16),
                   jax.ShapeDtypeStruct((t, 2 * SSD_HEADS), F32)],
        compiler_params=_cparams(("arbitrary",)),
        name="ssd_inproj",
    )(x, modrows, g, wz, wx, wdt)


_CONV_PAD = 8


def _ssd_conv_kernel(x_ref, w_ref, b_ref, o_ref, pad_ref, *, segments, rows):
    half = SSD_CONV // 2
    w = w_ref[...]
    bias = b_ref[...]
    cols = x_ref.shape[2]
    zeros = jnp.zeros((_CONV_PAD, cols), F32)
    for start, length in segments:
        pad_ref[0:_CONV_PAD, :] = zeros
        pad_ref[_CONV_PAD + length:2 * _CONV_PAD + length, :] = zeros
        for r in range(0, length, rows):
            pad_ref[_CONV_PAD + r:_CONV_PAD + r + rows, :] = (
                x_ref[0, start + r:start + r + rows, :].astype(F32))
        for r in range(0, length, rows):
            acc = None
            for k in range(SSD_CONV):
                lo = _CONV_PAD + r + k - half
                term = pad_ref[lo:lo + rows, :] * w[k:k + 1, :]
                acc = term if acc is None else acc + term
            o_ref[0, start + r:start + r + rows, :] = _silu(acc + bias).astype(o_ref.dtype)


def _ssd_conv(geom, xbc, conv_w, conv_b):
    cols = 512
    segments = ((0, geom.n_ctx), (geom.n_ctx, geom.n_seq))
    kern = functools.partial(_ssd_conv_kernel, segments=segments, rows=geom.tm)
    xbc3 = xbc.reshape(geom.batch, geom.nt, SSD_CONV_DIM)
    out = pl.pallas_call(
        kern,
        grid=(geom.batch, SSD_CONV_DIM // cols),
        in_specs=[pl.BlockSpec((1, geom.nt, cols), lambda b, j: (b, 0, j)),
                  pl.BlockSpec((SSD_CONV, cols), lambda b, j: (0, j)),
                  pl.BlockSpec((1, cols), lambda b, j: (0, j))],
        out_specs=pl.BlockSpec((1, geom.nt, cols), lambda b, j: (b, 0, j)),
        out_shape=jax.ShapeDtypeStruct(xbc3.shape, BF16),
        scratch_shapes=[pltpu.VMEM((max(geom.n_ctx, geom.n_seq) + 2 * _CONV_PAD, cols), F32)],
        compiler_params=_cparams(("arbitrary", "arbitrary")),
        name="ssd_conv",
    )(xbc3, conv_w, conv_b.reshape(1, SSD_CONV_DIM))
    return out


def _ssd_scan_kernel(xbc_ref, dt_ref, bias_ref, alog_ref, expand_ref, o_ref, state_ref):
    k = pl.program_id(1)
    j = pl.program_id(2)
    L, H, P, G, N = SSD_CHUNK, SSD_HEADS, SSD_HEADDIM, SSD_GROUPS, SSD_STATE
    hg = H // G

    @pl.when(j == 0)
    def _():
        state_ref[...] = jnp.zeros_like(state_ref)

    fwd = k == 0
    dt_all = dt_ref[0]
    dt_raw = jnp.where(fwd, dt_all[:, :H], dt_all[:, H:])
    bias = jnp.where(fwd, bias_ref[0:1, :], bias_ref[1:2, :])
    alog = jnp.where(fwd, alog_ref[0:1, :], alog_ref[1:2, :])
    v = dt_raw + bias
    dtk = jnp.maximum(v, 0.0) + jnp.log(1.0 + jnp.exp(-jnp.abs(v)))
    a = -jnp.exp(alog) * dtk

    row = lax.broadcasted_iota(jnp.int32, (L, L), 0)
    col = lax.broadcasted_iota(jnp.int32, (L, L), 1)
    ahead = jnp.where(fwd, row - col, col - row)
    incl = ahead >= 0
    tri = incl.astype(BF16)
    tri_t = (ahead <= 0).astype(BF16)
    a_cs = None
    a_cs_t = None
    for p in _split_bf16(a, 3):
        t1 = jnp.dot(tri, p, preferred_element_type=F32)
        t2 = lax.dot_general(p, tri_t, _TN, preferred_element_type=F32)
        a_cs = t1 if a_cs is None else a_cs + t1
        a_cs_t = t2 if a_cs_t is None else a_cs_t + t2
    total = jnp.sum(a, axis=0, keepdims=True)

    expand = expand_ref[...]
    dt_x = _dot_split(dtk, expand, 3)
    acs_x = _dot_split(a_cs, expand, 3)
    tot_x = _dot_split(total, expand, 3)

    xs = xbc_ref[0, :, 0:SSD_INNER].astype(F32)
    bm = xbc_ref[0, :, SSD_INNER:SSD_INNER + SSD_BC]
    cm = xbc_ref[0, :, SSD_INNER + SSD_BC:SSD_INNER + 2 * SSD_BC]
    xdt = xs * dt_x
    xdt_b = xdt.astype(BF16)
    z_b = (xdt * jnp.exp(tot_x - acs_x)).astype(BF16)
    decay_in = jnp.exp(acs_x)
    state = state_ref[...]
    state_b = state.astype(BF16)
    lane = lax.broadcasted_iota(jnp.int32, (L, 2 * P), 1)
    first_head = lane < P

    new_state = []
    for g in range(G):
        bg = bm[:, g * N:(g + 1) * N]
        cg = cm[:, g * N:(g + 1) * N]
        cb = lax.dot_general(cg, bg, _NT, preferred_element_type=F32)
        cols = slice(g * hg * P, (g + 1) * hg * P)
        y_off = jnp.dot(cg, state_b[:, cols], preferred_element_type=F32) * decay_in[:, cols]
        for hp in range(hg // 2):
            h0 = g * hg + 2 * hp
            gs = []
            for h in (h0, h0 + 1):
                seg = jnp.exp(jnp.where(incl, a_cs[:, h:h + 1] - a_cs_t[h:h + 1, :], -jnp.inf))
                gs.append((cb * seg).astype(BF16))
            lhs = jnp.concatenate(gs, axis=1)
            xp = xdt_b[:, h0 * P:(h0 + 2) * P]
            zero = jnp.zeros_like(xp)
            rhs = jnp.concatenate([jnp.where(first_head, xp, zero),
                                   jnp.where(first_head, zero, xp)], axis=0)
            y_pair = jnp.dot(lhs, rhs, preferred_element_type=F32)
            lo = 2 * hp * P
            o_ref[0, 0, :, h0 * P:(h0 + 2) * P] = y_pair + y_off[:, lo:lo + 2 * P]
        upd = lax.dot_general(bg, z_b[:, cols], _TN, preferred_element_type=F32)
        new_state.append(state[:, cols] * jnp.exp(tot_x[:, cols]) + upd)
    for g in range(G):
        state_ref[:, g * hg * P:(g + 1) * hg * P] = new_state[g]


def _ssd_scan(geom, xbcc, dt, dt_bias, a_log):
    nch = geom.nt // SSD_CHUNK
    cch = geom.n_ctx // SSD_CHUNK

    def chunk_index(k, j):
        rev = jnp.where(j < cch, cch - 1 - j, nch - 1 - (j - cch))
        return jnp.where(k == 0, j, rev)

    expand = (jnp.arange(SSD_HEADS)[:, None] == (jnp.arange(SSD_INNER)[None, :] // SSD_HEADDIM)).astype(BF16)
    dt3 = dt.reshape(geom.batch, geom.nt, 2 * SSD_HEADS)
    return pl.pallas_call(
        _ssd_scan_kernel,
        grid=(geom.batch, 2, nch),
        in_specs=[pl.BlockSpec((1, SSD_CHUNK, SSD_CONV_DIM), lambda b, k, j: (b, chunk_index(k, j), 0)),
                  pl.BlockSpec((1, SSD_CHUNK, 2 * SSD_HEADS), lambda b, k, j: (b, chunk_index(k, j), 0)),
                  pl.BlockSpec((2, SSD_HEADS), lambda b, k, j: (0, 0)),
                  pl.BlockSpec((2, SSD_HEADS), lambda b, k, j: (0, 0)),
                  pl.BlockSpec((SSD_HEADS, SSD_INNER), lambda b, k, j: (0, 0))],
        out_specs=pl.BlockSpec((1, 1, SSD_CHUNK, SSD_INNER), lambda b, k, j: (k, b, chunk_index(k, j), 0)),
        out_shape=jax.ShapeDtypeStruct((2, geom.batch, geom.nt, SSD_INNER), F32),
        scratch_shapes=[pltpu.VMEM((SSD_STATE, SSD_INNER), F32)],
        compiler_params=_cparams(("arbitrary", "arbitrary", "arbitrary")),
        name="ssd_scan",
    )(xbcc, dt3, dt_bias, a_log, expand)


def _route(h2, rwhi_ref, rwlo_ref, rb_ref, rt_ref):
    tm = h2.shape[0]
    h_hi, h_lo = _split_bf16(h2, 2)
    rw_hi, rw_lo = rwhi_ref[...], rwlo_ref[...]
    logits = (lax.dot_general(rw_hi, h_hi, _NT, preferred_element_type=F32)
              + lax.dot_general(rw_hi, h_lo, _NT, preferred_element_type=F32)
              + lax.dot_general(rw_lo, h_hi, _NT, preferred_element_type=F32))
    scores = _sigmoid(logits)
    biased = scores + rb_ref[...]
    s = [scores[e:e + 1, :] for e in range(N_EXPERTS)]
    b = [biased[e:e + 1, :] for e in range(N_EXPERTS)]

    def pair_max(v):
        m = v[0] + v[1]
        for lo, hi in zip(PAIR_LO[1:], PAIR_HI[1:]):
            m = jnp.maximum(m, v[lo] + v[hi])
        return m

    gsc = [pair_max(b[4 * g:4 * g + 4]) for g in range(N_GROUPS)]
    gbest = jnp.zeros((1, tm), jnp.int32)
    best = gsc[0]
    for g in range(1, N_GROUPS):
        take = gsc[g] > best
        gbest = jnp.where(take, g, gbest)
        best = jnp.where(take, gsc[g], best)

    def pick(vals, j):
        out = vals[j]
        for g in range(1, N_GROUPS):
            out = jnp.where(gbest == g, vals[4 * g + j], out)
        return out

    vb = [pick(b, j) for j in range(EXPERTS_PER_GROUP)]
    vs = [pick(s, j) for j in range(EXPERTS_PER_GROUP)]

    def argmax4(v):
        idx = jnp.zeros((1, tm), jnp.int32)
        m = v[0]
        for j in range(1, EXPERTS_PER_GROUP):
            take = v[j] > m
            idx = jnp.where(take, j, idx)
            m = jnp.where(take, v[j], m)
        return idx

    i1 = argmax4(vb)
    i2 = argmax4([jnp.where(i1 == j, -jnp.inf, vb[j]) for j in range(EXPERTS_PER_GROUP)])
    s1 = sum(jnp.where(i1 == j, vs[j], 0.0) for j in range(EXPERTS_PER_GROUP))
    s2 = sum(jnp.where(i2 == j, vs[j], 0.0) for j in range(EXPERTS_PER_GROUP))
    w1 = s1 / (s1 + s2)
    w2 = s2 / (s1 + s2)
    lo = jnp.minimum(i1, i2)
    hi = jnp.maximum(i1, i2)
    pair = jnp.where(lo == 0, hi - 1, jnp.where(lo == 1, hi + 1, 5))
    combo = gbest * PAIRS_PER_GROUP + pair
    first_lo = i1 < i2
    rt_ref[0:1, :] = combo.astype(F32)
    rt_ref[1:2, :] = jnp.where(first_lo, w1, w2)
    rt_ref[2:3, :] = jnp.where(first_lo, w2, w1)
    rt_ref[3:8, :] = jnp.zeros((5, tm), F32)


def _residual_route(x, y, mod, n2g, rwhi_ref, rwlo_ref, rb_ref, xo_ref, h2_ref, rt_ref):
    d = D_MODEL
    x_new = x + mod[:, 2 * d:3 * d] * y
    xo_ref[...] = x_new
    h2 = _norm_mod(x_new, n2g, mod[:, 3 * d:4 * d], mod[:, 4 * d:5 * d])
    h2_ref[...] = h2
    _route(h2, rwhi_ref, rwlo_ref, rb_ref, rt_ref)


def _ssd_out_kernel(yd_ref, xs_ref, z_ref, x_ref, mod_ref, dsk_ref, ng_ref, w_ref, n2g_ref,
                    rwhi_ref, rwlo_ref, rb_ref, xo_ref, h2_ref, rt_ref):
    y = dsk_ref[...] * xs_ref[...].astype(F32) + yd_ref[0] + yd_ref[1]
    y = y * _silu(z_ref[...].astype(F32))
    yn = (y * _rms_scale(y, SSD_INNER) * ng_ref[...]).astype(BF16)
    out = jnp.dot(yn, w_ref[...], preferred_element_type=F32)
    _residual_route(x_ref[...], out, mod_ref[0], n2g_ref[...], rwhi_ref, rwlo_ref, rb_ref,
                    xo_ref, h2_ref, rt_ref)


def _route_specs(geom):
    in_specs = [_full_spec((1, D_MODEL)), _full_spec((N_EXPERTS, D_MODEL)),
                _full_spec((N_EXPERTS, D_MODEL)), _full_spec((N_EXPERTS, 1))]
    out_specs = [_row_spec(geom, D_MODEL), _row_spec(geom, D_MODEL),
                 pl.BlockSpec((8, geom.tm), lambda i: (i, 0))]
    out_shape = [jax.ShapeDtypeStruct((geom.t, D_MODEL), F32),
                 jax.ShapeDtypeStruct((geom.t, D_MODEL), F32),
                 jax.ShapeDtypeStruct((geom.n_tiles * 8, geom.tm), F32)]
    return in_specs, out_specs, out_shape


def _ssd_out(geom, ydir, xbcc, z, x, modrows, dskip_x, norm_g, w_out, n2g, rw_hi, rw_lo, rb):
    r_in, r_out, r_shape = _route_specs(geom)
    tm = geom.tm
    yd2 = ydir.reshape(2, geom.t, SSD_INNER)
    xbcc2 = xbcc.reshape(geom.t, SSD_CONV_DIM)
    return pl.pallas_call(
        _ssd_out_kernel,
        grid=(geom.n_tiles,),
        in_specs=[pl.BlockSpec((2, tm, SSD_INNER), lambda i: (0, i, 0)),
                  pl.BlockSpec((tm, SSD_INNER), lambda i: (i, 0)),
                  _row_spec(geom, SSD_INNER), _row_spec(geom, D_MODEL), _mod_spec(geom),
                  _full_spec((1, SSD_INNER)), _full_spec((1, SSD_INNER)),
                  _full_spec(w_out.shape)] + r_in,
        out_specs=r_out,
        out_shape=r_shape,
        compiler_params=_cparams(("arbitrary",)),
        name="ssd_out",
    )(yd2, xbcc2, z, x, modrows, dskip_x, norm_g, w_out, n2g, rw_hi, rw_lo, rb)


def _da_inproj_kernel(x_ref, mod_ref, g_ref, wqt_ref, wk_ref, wvt_ref, gqx_ref, gk_ref,
                      cos_ref, sin_ref, cost_ref, sint_ref, gsum_ref, gexp_ref,
                      qt_ref, k_ref, vt_ref):
    d = D_MODEL
    mod = mod_ref[0]
    h = _norm_mod(x_ref[...], g_ref[...], mod[:, 0:d], mod[:, d:2 * d]).astype(BF16)
    tm = h.shape[0]

    vt_ref[0, 0] = lax.dot_general(wvt_ref[...], h, _NT, preferred_element_type=F32).astype(BF16)

    yt = lax.dot_general(wqt_ref[...], h, _NT, preferred_element_type=F32)
    cost = cost_ref[...]
    sint = sint_ref[...]
    gqx = gqx_ref[...]
    for hd in range(DA_HEADS):
        y3 = yt[hd * LANES:(hd + 1) * LANES, :].reshape(2, DA_HEAD_DIM, tm)
        ss = jnp.sum(y3 * y3, axis=1, keepdims=True)
        yn = (y3 * lax.rsqrt(ss * (1.0 / DA_HEAD_DIM) + RMS_EPS)).reshape(LANES, tm) * gqx
        y4 = yn.reshape(4, 2, 16, tm)
        partner = jnp.concatenate([y4[:, 1:2], y4[:, 0:1]], axis=1).reshape(LANES, tm)
        qt_ref[0, 0, hd * LANES:(hd + 1) * LANES, :] = (yn * cost + partner * sint).astype(BF16)

    cos = cos_ref[...]
    sin = sin_ref[...]
    lane = lax.broadcasted_iota(jnp.int32, cos.shape, 1)
    first_half = (lane % 32) < 16
    y = jnp.dot(h, wk_ref[...], preferred_element_type=F32)
    ss = jnp.dot((y * y).astype(BF16), gsum_ref[...], preferred_element_type=F32)
    r = lax.rsqrt(ss * (1.0 / DA_HEAD_DIM) + RMS_EPS)
    rx = _dot_split(r, gexp_ref[...], 2)
    yn = y * rx * gk_ref[...]
    for hd in range(DA_HEADS):
        c = yn[:, hd * LANES:(hd + 1) * LANES]
        partner = jnp.where(first_half, pltpu.roll(c, LANES - 16, 1), pltpu.roll(c, 16, 1))
        k_ref[:, hd * LANES:(hd + 1) * LANES] = (c * cos + partner * sin).astype(k_ref.dtype)


def _da_inproj(geom, x, modrows, g, wqt, wk, wvt, gqx, gk, cos_t, sin_t, gsum, gexp):
    t, tm = geom.t, geom.tm
    tpb = geom.tiles_per_batch
    tab_spec = pl.BlockSpec((tm, LANES), lambda i: (i % tpb, 0))
    tabt_spec = pl.BlockSpec((LANES, tm), lambda i: (0, i % tpb))
    tr_spec = pl.BlockSpec((1, 1, D_MODEL, tm), lambda i: (i // tpb, i % tpb, 0, 0))
    tr_shape = jax.ShapeDtypeStruct((geom.batch, tpb, D_MODEL, tm), BF16)
    return pl.pallas_call(
        _da_inproj_kernel,
        grid=(geom.n_tiles,),
        in_specs=[_row_spec(geom, D_MODEL), _mod_spec(geom), _full_spec((1, D_MODEL)),
                  _full_spec(wqt.shape), _full_spec(wk.shape), _full_spec(wvt.shape),
                  _full_spec((LANES, tm)), _full_spec((1, D_MODEL)), tab_spec, tab_spec,
                  tabt_spec, tabt_spec, _full_spec(gsum.shape), _full_spec(gexp.shape)],
        out_specs=[tr_spec, _row_spec(geom, D_MODEL), tr_spec],
        out_shape=[tr_shape, jax.ShapeDtypeStruct((t, D_MODEL), BF16), tr_shape],
        compiler_params=_cparams(("arbitrary",)),
        name="da_inproj",
    )(x, modrows, g, wqt, wk, wvt, gqx, gk, cos_t, sin_t, cos_t.T, sin_t.T, gsum, gexp)


def _da_attn_kernel(qt_ref, k_ref, vt_ref, lq_ref, lk_ref, sgx_ref, o_ref,
                    s0_ref, s1_ref, acc0_ref, acc1_ref, *, ctx_tiles, all_tiles, lam_init):
    qi = pl.program_id(2)
    qt = qt_ref[0, 0]
    tk = vt_ref.shape[3]
    row = lax.broadcasted_iota(jnp.int32, qt.shape, 0)
    zero = jnp.zeros_like(qt)
    qm = (jnp.where(row < DA_HEAD_DIM, qt, zero), jnp.where(row < DA_HEAD_DIM, zero, qt))
    tq = qt.shape[1]
    n_chunks = jnp.where(qi < ctx_tiles, ctx_tiles, all_tiles)
    acc_refs = (acc0_ref, acc1_ref)
    for r in acc_refs:
        r[...] = jnp.zeros(r.shape, F32)

    def scores(j, s_ref):
        off = pl.multiple_of(j * tk, tk)
        kc = k_ref[0, pl.ds(off, tk), :]
        for m in range(2):
            s_ref[m] = jnp.dot(kc, qm[m], preferred_element_type=F32)

    def softmax(s_ref, stats):
        probs, new_stats = [], []
        for m in range(2):
            m_old, l_old = stats[m]
            st = s_ref[m]
            m_new = jnp.maximum(m_old, jnp.max(st, axis=0, keepdims=True))
            alpha = jnp.exp2(m_old - m_new)
            pt = jnp.exp2(st - m_new)
            l_new = alpha * l_old + jnp.sum(pt, axis=0, keepdims=True)
            probs.append((alpha, pt.astype(BF16)))
            new_stats.append((m_new, l_new))
        return probs, tuple(new_stats)

    def attend(j, probs):
        vtc = vt_ref[0, j]
        for m in range(2):
            alpha, pt = probs[m]
            acc_refs[m][...] = alpha * acc_refs[m][...] + jnp.dot(vtc, pt, preferred_element_type=F32)

    scores(0, s0_ref)

    def body(i, stats):
        a = 2 * i
        scores(a + 1, s1_ref)
        pa, stats = softmax(s0_ref, stats)
        scores(a + 2, s0_ref)
        attend(a, pa)
        pb, stats = softmax(s1_ref, stats)
        attend(a + 1, pb)
        return stats

    neg = jnp.full((1, tq), -1e30, F32)
    zl = jnp.zeros((1, tq), F32)
    stats = lax.fori_loop(0, (n_chunks - 1) // 2, body, ((neg, zl), (neg, zl)))
    p_last, stats = softmax(s0_ref, stats)
    attend(n_chunks - 1, p_last)

    tdot = jnp.sum(lq_ref[...] * lk_ref[...], axis=-1, keepdims=True)
    e = jnp.exp(tdot)
    lam = e[0:1, :] - e[1:2, :] + lam_init
    ot = acc0_ref[...] / stats[0][1] - lam * (acc1_ref[...] / stats[1][1])
    ms = jnp.sum(ot * ot, axis=0, keepdims=True) * (1.0 / DA_V_DIM)
    ot = ot * lax.rsqrt(ms + RMS_EPS) * sgx_ref[...]
    o_ref[0] = ot.T.astype(o_ref.dtype)


def _da_attn(geom, qt, k, vt, lam_q, lam_k, sub_gx, lam_init):
    tq = geom.tm
    b, nt, tpb = geom.batch, geom.nt, geom.tiles_per_batch
    k3 = k.reshape(b, nt, D_MODEL)
    kern = functools.partial(_da_attn_kernel, ctx_tiles=geom.ctx_tiles, all_tiles=tpb,
                             lam_init=lam_init)
    small = lambda shape: pl.BlockSpec(shape, lambda bi, h, i: (0, 0))
    return pl.pallas_call(
        kern,
        grid=(b, DA_HEADS, tpb),
        in_specs=[pl.BlockSpec((1, 1, LANES, tq), lambda bi, h, i: (bi, i, h, 0)),
                  pl.BlockSpec((1, nt, LANES), lambda bi, h, i: (bi, 0, h)),
                  pl.BlockSpec((1, tpb, LANES, tq), lambda bi, h, i: (bi, 0, h, 0)),
                  small((2, DA_HEAD_DIM)), small((2, DA_HEAD_DIM)), small((DA_V_DIM, tq))],
        out_specs=pl.BlockSpec((1, tq, LANES), lambda bi, h, i: (bi, i, h)),
        out_shape=jax.ShapeDtypeStruct((b, nt, D_MODEL), BF16),
        scratch_shapes=[pltpu.VMEM((2, tq, tq), F32), pltpu.VMEM((2, tq, tq), F32),
                        pltpu.VMEM((DA_V_DIM, tq), F32), pltpu.VMEM((DA_V_DIM, tq), F32)],
        compiler_params=_cparams(("arbitrary", "arbitrary", "arbitrary")),
        name="da_attn",
    )(qt, k3, vt, lam_q, lam_k, sub_gx)


def _da_out_kernel(o_ref, x_ref, mod_ref, w_ref, n2g_ref, rwhi_ref, rwlo_ref, rb_ref,
                   xo_ref, h2_ref, rt_ref):
    out = jnp.dot(o_ref[...], w_ref[...], preferred_element_type=F32)
    _residual_route(x_ref[...], out, mod_ref[0], n2g_ref[...], rwhi_ref, rwlo_ref, rb_ref,
                    xo_ref, h2_ref, rt_ref)


def _da_out(geom, o, x, modrows, w_out, n2g, rw_hi, rw_lo, rb):
    r_in, r_out, r_shape = _route_specs(geom)
    return pl.pallas_call(
        _da_out_kernel,
        grid=(geom.n_tiles,),
        in_specs=[_row_spec(geom, D_MODEL), _row_spec(geom, D_MODEL), _mod_spec(geom),
                  _full_spec(w_out.shape)] + r_in,
        out_specs=r_out,
        out_shape=r_shape,
        compiler_params=_cparams(("arbitrary",)),
        name="da_out",
    )(o.reshape(geom.t, D_MODEL), x, modrows, w_out, n2g, rw_hi, rw_lo, rb)


def _gather_rows(idx_ref, base, src_hbm, dst, sem, rows):
    def body(r, carry):
        t = idx_ref[base + r]
        pltpu.make_async_copy(src_hbm.at[pl.ds(t, 1), :], dst.at[pl.ds(r, 1), :], sem).start()
        return carry
    lax.fori_loop(0, rows, body, 0)


def _wait_rows(src_hbm, dst, sem, rows):
    pltpu.make_async_copy(src_hbm.at[pl.ds(0, rows), :], dst, sem).wait()


def _moe_kernel(e0_ref, e1_ref, used_ref, tok_ref, h2_hbm, wt_ref,
                wg0_ref, wu0_ref, wd0_ref, wg1_ref, wu1_ref, wd1_ref, y_ref, xbuf, sem):
    del e0_ref, e1_ref
    i = pl.program_id(0)
    tm = y_ref.shape[0]
    n_used = used_ref[0]
    slot = i % 2

    @pl.when(jnp.logical_and(i == 0, n_used > 0))
    def _():
        _gather_rows(tok_ref, 0, h2_hbm, xbuf.at[0], sem.at[0], tm)

    @pl.when(i + 1 < n_used)
    def _():
        _gather_rows(tok_ref, (i + 1) * tm, h2_hbm, xbuf.at[1 - slot], sem.at[1 - slot], tm)

    @pl.when(i < n_used)
    def _():
        _wait_rows(h2_hbm, xbuf.at[slot], sem.at[slot], tm)
        x = xbuf[slot].astype(BF16)
        wt = wt_ref[...]
        y = None
        for e, (wg, wu, wd) in enumerate(((wg0_ref, wu0_ref, wd0_ref), (wg1_ref, wu1_ref, wd1_ref))):
            gate = jnp.dot(x, wg[0], preferred_element_type=F32)
            up = jnp.dot(x, wu[0], preferred_element_type=F32)
            hid = (_silu(gate) * up).astype(BF16)
            ye = jnp.dot(hid, wd[0], preferred_element_type=F32) * wt[:, e:e + 1]
            y = ye if y is None else y + ye
        y_ref[...] = y

    @pl.when(i >= n_used)
    def _():
        y_ref[...] = jnp.zeros_like(y_ref)


def _moe_plan(geom, route):
    tm, t = geom.tm, geom.t
    rt = route.reshape(geom.n_tiles, 8, tm)
    combo = rt[:, 0, :].reshape(t).astype(jnp.int32)
    w_lo = rt[:, 1, :].reshape(t)
    w_hi = rt[:, 2, :].reshape(t)
    onehot = (combo[:, None] == jnp.arange(N_COMBOS, dtype=jnp.int32)[None, :]).astype(jnp.int32)
    csum = jnp.cumsum(onehot, axis=0)
    counts = csum[-1]
    rank = jnp.sum(csum * onehot, axis=1) - 1
    padded = ((counts + tm - 1) // tm) * tm
    ends = jnp.cumsum(padded)
    offs = ends - padded
    pos = offs[combo] + rank
    p_rows = t + N_COMBOS * tm
    n_ptiles = p_rows // tm
    tok_sorted = jnp.zeros((p_rows,), jnp.int32).at[pos].set(jnp.arange(t, dtype=jnp.int32))
    wt_sorted = jnp.zeros((p_rows, 2), F32).at[pos].set(jnp.stack([w_lo, w_hi], axis=1))
    n_used = (ends[-1] // tm).astype(jnp.int32)
    tile_start = jnp.arange(n_ptiles, dtype=jnp.int32) * tm
    last_start = jnp.maximum(n_used - 1, 0) * tm
    tile_combo = jnp.searchsorted(ends, jnp.minimum(tile_start, last_start), side="right")
    tile_combo = jnp.minimum(tile_combo, N_COMBOS - 1).astype(jnp.int32)
    grp = tile_combo // PAIRS_PER_GROUP
    pair = tile_combo % PAIRS_PER_GROUP
    e0 = grp * EXPERTS_PER_GROUP + jnp.asarray(PAIR_LO, jnp.int32)[pair]
    e1 = grp * EXPERTS_PER_GROUP + jnp.asarray(PAIR_HI, jnp.int32)[pair]
    return e0, e1, n_used.reshape(1), tok_sorted, wt_sorted, pos


def _moe_experts(geom, h2, plan, wg, wu, wd):
    e0, e1, n_used, tok_sorted, wt_sorted, _ = plan
    tm = geom.tm
    p_rows = tok_sorted.shape[0]
    de = wg.shape[2]

    def wspec(shape, which):
        if which == 0:
            return pl.BlockSpec(shape, lambda i, e0r, e1r, ur, tr: (e0r[i], 0, 0))
        return pl.BlockSpec(shape, lambda i, e0r, e1r, ur, tr: (e1r[i], 0, 0))

    grid_spec = pltpu.PrefetchScalarGridSpec(
        num_scalar_prefetch=4,
        grid=(p_rows // tm,),
        in_specs=[pl.BlockSpec(memory_space=pl.ANY),
                  pl.BlockSpec((tm, 2), lambda i, *_: (i, 0)),
                  wspec((1, D_MODEL, de), 0), wspec((1, D_MODEL, de), 0), wspec((1, de, D_MODEL), 0),
                  wspec((1, D_MODEL, de), 1), wspec((1, D_MODEL, de), 1), wspec((1, de, D_MODEL), 1)],
        out_specs=pl.BlockSpec((tm, D_MODEL), lambda i, *_: (i, 0)),
        scratch_shapes=[pltpu.VMEM((2, tm, D_MODEL), F32), pltpu.SemaphoreType.DMA((2,))],
    )
    return pl.pallas_call(
        _moe_kernel,
        grid_spec=grid_spec,
        out_shape=jax.ShapeDtypeStruct((p_rows, D_MODEL), F32),
        compiler_params=_cparams(("arbitrary",)),
        name="moe_experts",
    )(e0, e1, n_used, tok_sorted, h2, wt_sorted, wg, wu, wd, wg, wu, wd)


def _moe_combine_kernel(pos_ref, x_ref, mod_ref, y_hbm, o_ref, ybuf, sem):
    i = pl.program_id(0)
    n = pl.num_programs(0)
    tm = o_ref.shape[0]
    slot = i % 2
    d = D_MODEL

    @pl.when(i == 0)
    def _():
        _gather_rows(pos_ref, 0, y_hbm, ybuf.at[0], sem.at[0], tm)

    @pl.when(i + 1 < n)
    def _():
        _gather_rows(pos_ref, (i + 1) * tm, y_hbm, ybuf.at[1 - slot], sem.at[1 - slot], tm)

    _wait_rows(y_hbm, ybuf.at[slot], sem.at[slot], tm)
    mod = mod_ref[0]
    o_ref[...] = x_ref[...] + mod[:, 5 * d:6 * d] * ybuf[slot]


def _moe_combine(geom, x, modrows, y_sorted, pos):
    tm = geom.tm
    grid_spec = pltpu.PrefetchScalarGridSpec(
        num_scalar_prefetch=1,
        grid=(geom.n_tiles,),
        in_specs=[pl.BlockSpec((tm, D_MODEL), lambda i, p: (i, 0)),
                  pl.BlockSpec((1, 1, 6 * D_MODEL), lambda i, p: (geom.mod_index(i), 0, 0)),
                  pl.BlockSpec(memory_space=pl.ANY)],
        out_specs=pl.BlockSpec((tm, D_MODEL), lambda i, p: (i, 0)),
        scratch_shapes=[pltpu.VMEM((2, tm, D_MODEL), F32), pltpu.SemaphoreType.DMA((2,))],
    )
    return pl.pallas_call(
        _moe_combine_kernel,
        grid_spec=grid_spec,
        out_shape=jax.ShapeDtypeStruct((geom.t, D_MODEL), F32),
        compiler_params=_cparams(("arbitrary",)),
        name="moe_combine",
    )(pos, x, modrows, y_sorted)


def _rope_tables(geom):
    tpos = jnp.arange(geom.n_seq, dtype=jnp.int32)
    pos = jnp.stack([tpos // GRID_W, tpos % GRID_W], axis=-1).astype(F32)
    n_freq = DA_HEAD_DIM // 4
    inv_freq = ROPE_BASE ** (-jnp.arange(n_freq, dtype=F32) / n_freq)
    ang = pos[..., None] * inv_freq
    cos, sin = jnp.cos(ang), jnp.sin(ang)
    cos_l = jnp.broadcast_to(cos[:, None, :, None, :], (geom.n_seq, 2, 2, 2, n_freq))
    sgn = jnp.asarray([-1.0, 1.0], F32)[None, None, None, :, None]
    sin_l = jnp.broadcast_to(sin[:, None, :, None, :], (geom.n_seq, 2, 2, 2, n_freq)) * sgn
    cos_l = cos_l.reshape(geom.n_seq, LANES)
    sin_l = sin_l.reshape(geom.n_seq, LANES)
    cos_t = jnp.concatenate([jnp.ones((geom.n_ctx, LANES), F32), cos_l], axis=0)
    sin_t = jnp.concatenate([jnp.zeros((geom.n_ctx, LANES), F32), sin_l], axis=0)
    return cos_t, sin_t


def _head_major(w):
    return w.reshape(D_MODEL, 2, DA_HEADS, DA_HEAD_DIM).transpose(0, 2, 1, 3).reshape(D_MODEL, D_MODEL)


def kernel(x, c, ctx, c_ctx, ada_w, ada_b, norm1_g, norm2_g, ssd_w_in, ssd_conv_w, ssd_conv_b,
           ssd_dt_bias, ssd_a_log, ssd_d, ssd_norm_g, ssd_w_out, da_w_in, da_q_norm, da_k_norm,
           da_lam_q, da_lam_k, da_sub_norm, da_w_out, router_w, router_b, moe_w_gate, moe_w_up,
           moe_w_down):
    batch, n_seq, d = x.shape
    n_ctx = ctx.shape[1]
    depth = ada_w.shape[0]
    assert d == D_MODEL and batch + 1 <= MOD_ROWS
    geom = _Geom(batch, n_ctx, n_seq)

    cvecs = jnp.zeros((MOD_ROWS, d), F32).at[:batch].set(c).at[batch].set(c_ctx)
    mod_all = _ada_mod(cvecs, ada_w, ada_b)

    xs = jnp.concatenate([ctx, x], axis=1).reshape(geom.t, d)

    rw_t = router_w.T
    rw_hi = rw_t.astype(BF16)
    rw_lo = (rw_t - rw_hi.astype(F32)).astype(BF16)
    rb = router_b.reshape(N_EXPERTS, 1)

    cos_t, sin_t = _rope_tables(geom)
    grp_of_col = jnp.arange(D_MODEL) // DA_HEAD_DIM
    gsum = (grp_of_col[:, None] == jnp.arange(LANES)[None, :]).astype(BF16)
    gexp = gsum.T

    for i in range(depth):
        mod_i = mod_all[i]
        modrows = jnp.stack([jnp.broadcast_to(mod_i[batch], (batch, 6 * d)), mod_i[:batch]],
                            axis=1).reshape(2 * batch, 1, 6 * d)
        g1 = norm1_g[i].reshape(1, d)
        g2 = norm2_g[i].reshape(1, d)
        j = i // 2
        if i % 2 == 0:
            w_in = ssd_w_in[j].astype(BF16)
            wz = w_in[:, :SSD_INNER]
            wx = w_in[:, SSD_INNER:SSD_INNER + SSD_CONV_DIM]
            wdt = w_in[:, SSD_INNER + SSD_CONV_DIM:]
            z, xbc, dt = _ssd_inproj(geom, xs, modrows, g1, wz, wx, wdt)
            xbcc = _ssd_conv(geom, xbc, ssd_conv_w[j], ssd_conv_b[j])
            ydir = _ssd_scan(geom, xbcc, dt, ssd_dt_bias[j], ssd_a_log[j])
            dskip_x = jnp.repeat(ssd_d[j], SSD_HEADDIM).reshape(1, SSD_INNER)
            xs, h2, route = _ssd_out(geom, ydir, xbcc, z, xs, modrows, dskip_x,
                                     ssd_norm_g[j].reshape(1, SSD_INNER),
                                     ssd_w_out[j].astype(BF16), g2, rw_hi, rw_lo, rb)
        else:
            lam_init = 0.8 - 0.6 * math.exp(-0.3 * i)
            w_in = da_w_in[j]
            wqt = _head_major(w_in[:, :D_MODEL]).T.astype(BF16)
            wk = _head_major(w_in[:, D_MODEL:2 * D_MODEL]).astype(BF16)
            wvt = w_in[:, 2 * D_MODEL:].T.astype(BF16)
            q_scale = DA_HEAD_DIM ** -0.5 * LOG2E
            gqx = jnp.broadcast_to((da_q_norm[j].reshape(LANES) * q_scale)[:, None], (LANES, geom.tm))
            gk = jnp.tile(da_k_norm[j].reshape(2 * DA_HEAD_DIM), DA_HEADS).reshape(1, d)
            qt, k, vt = _da_inproj(geom, xs, modrows, g1, wqt, wk, wvt, gqx, gk, cos_t, sin_t,
                                   gsum, gexp)
            sub_gx = jnp.broadcast_to((da_sub_norm[j] * (1.0 - lam_init))[:, None],
                                      (DA_V_DIM, geom.tm))
            o = _da_attn(geom, qt, k, vt, da_lam_q[j], da_lam_k[j], sub_gx, lam_init)
            xs, h2, route = _da_out(geom, o, xs, modrows, da_w_out[j].astype(BF16), g2,
                                    rw_hi, rw_lo, rb)
        plan = _moe_plan(geom, route)
        y_sorted = _moe_experts(geom, h2, plan, moe_w_gate[i].astype(BF16),
                                moe_w_up[i].astype(BF16), moe_w_down[i].astype(BF16))
        xs = _moe_combine(geom, xs, modrows, y_sorted, plan[5])

    return xs.reshape(batch, geom.nt, d)[:, n_ctx:, :]
```

```python
import functools
import math

import jax
import jax.numpy as jnp
from jax import lax
from jax.experimental import pallas as pl
from jax.experimental.pallas import tpu as pltpu

F32 = jnp.float32
BF16 = jnp.bfloat16

D_MODEL = 1024
DEPTH = 4
GRID_W = 64
RMS_EPS = 1e-6

SSD_INNER = 2048
SSD_HEADDIM = 64
SSD_HEADS = 32
SSD_GROUPS = 4
SSD_STATE = 128
SSD_CONV = 5
SSD_CHUNK = 128
SSD_BC = SSD_GROUPS * SSD_STATE
SSD_CONV_DIM = SSD_INNER + 2 * SSD_BC

DA_HEADS = 8
DA_HEAD_DIM = 64
DA_V_DIM = 128
ROPE_BASE = 10000.0

N_EXPERTS = 16
N_GROUPS = 4
EXPERTS_PER_GROUP = 4
D_EXPERT = 512
PAIRS_PER_GROUP = 6
N_COMBOS = N_GROUPS * PAIRS_PER_GROUP
PAIR_LO = (0, 0, 0, 1, 1, 2)
PAIR_HI = (1, 2, 3, 2, 3, 3)
ROUTE_ROWS = 32
MOE_ROW = D_MODEL + 128

LANES = 128
MOD_ROWS = 16
VMEM_LIMIT = 56 * 1024 * 1024

LOG2E = 1.4426950408889634


def _cparams(sem):
    return pltpu.CompilerParams(dimension_semantics=sem, vmem_limit_bytes=VMEM_LIMIT)


def _split_bf16(a, n):
    parts = []
    r = a
    for _ in range(n):
        p = r.astype(BF16)
        parts.append(p)
        r = r - p.astype(F32)
    return parts


def _dot_split(a, b_bf16, n, dims=(((1,), (0,)), ((), ()))):
    out = None
    for p in _split_bf16(a, n):
        t = lax.dot_general(p, b_bf16, dims, preferred_element_type=F32)
        out = t if out is None else out + t
    return out


_NT = (((1,), (1,)), ((), ()))
_TN = (((0,), (0,)), ((), ()))


def _sigmoid(x):
    return 1.0 / (1.0 + jnp.exp(-x))


def _silu(x):
    return x * _sigmoid(x)


def _rms_scale(x, n):
    return lax.rsqrt(jnp.sum(x * x, axis=-1, keepdims=True) * (1.0 / n) + RMS_EPS)


def _norm_mod(x, g, shift, scale):
    y = x * _rms_scale(x, x.shape[-1]) * g
    return y * (1.0 + scale) + shift


def _ada_kernel(c_ref, w_ref, b_ref, o_ref):
    s = _silu(c_ref[...])
    acc = jnp.dot(s, w_ref[0], precision=lax.Precision.HIGHEST, preferred_element_type=F32)
    o_ref[0] = acc + b_ref[0]


def _ada_mod(cvecs, ada_w, ada_b):
    depth, d, n6 = ada_w.shape
    tn = 1536
    return pl.pallas_call(
        _ada_kernel,
        grid=(depth, n6 // tn),
        in_specs=[pl.BlockSpec((MOD_ROWS, d), lambda i, j: (0, 0)),
                  pl.BlockSpec((1, d, tn), lambda i, j: (i, 0, j)),
                  pl.BlockSpec((1, 1, tn), lambda i, j: (i, 0, j))],
        out_specs=pl.BlockSpec((1, MOD_ROWS, tn), lambda i, j: (i, 0, j)),
        out_shape=jax.ShapeDtypeStruct((depth, MOD_ROWS, n6), F32),
        compiler_params=_cparams(("arbitrary", "arbitrary")),
        name="ada_mod",
    )(cvecs, ada_w, ada_b.reshape(depth, 1, n6))


class _Geom:
    def __init__(self, batch, n_ctx, n_seq):
        self.batch, self.n_ctx, self.n_seq = batch, n_ctx, n_seq
        self.nt = n_ctx + n_seq
        self.t = batch * self.nt
        self.tm = 256 if (n_ctx % 256 == 0 and n_seq % 256 == 0) else 128
        self.tiles_per_batch = self.nt // self.tm
        self.ctx_tiles = n_ctx // self.tm
        self.n_tiles = self.t // self.tm

    def mod_index(self, i):
        b = i // self.tiles_per_batch
        r = i % self.tiles_per_batch
        return 2 * b + (r >= self.ctx_tiles).astype(jnp.int32)


def _mod_spec(geom):
    return pl.BlockSpec((1, 1, 6 * D_MODEL), lambda i: (geom.mod_index(i), 0, 0))


def _row_spec(geom, width):
    return pl.BlockSpec((geom.tm, width), lambda i: (i, 0))


def _full_spec(shape):
    return pl.BlockSpec(shape, lambda i: (0,) * len(shape))


def _chunked_dot_store(h, w_ref, o_ref, chunk=512):
    n = w_ref.shape[1]
    for j in range(0, n, chunk):
        c = min(chunk, n - j)
        o_ref[:, j:j + c] = jnp.dot(h, w_ref[:, j:j + c],
                                    preferred_element_type=F32).astype(o_ref.dtype)


def _ssd_inproj_kernel(x_ref, mod_ref, g_ref, wz_ref, wx_ref, wdt_ref, z_ref, xbc_ref, dt_ref):
    d = D_MODEL
    mod = mod_ref[0]
    h = _norm_mod(x_ref[...], g_ref[...], mod[:, 0:d], mod[:, d:2 * d]).astype(BF16)
    _chunked_dot_store(h, wz_ref, z_ref)
    _chunked_dot_store(h, wx_ref, xbc_ref)
    dt_ref[...] = jnp.dot(h, wdt_ref[...], preferred_element_type=F32)


def _ssd_inproj(geom, x, modrows, g, wz, wx, wdt):
    t = geom.t
    return pl.pallas_call(
        _ssd_inproj_kernel,
        grid=(geom.n_tiles,),
        in_specs=[_row_spec(geom, D_MODEL), _mod_spec(geom), _full_spec((1, D_MODEL)),
                  _full_spec(wz.shape), _full_spec(wx.shape), _full_spec(wdt.shape)],
        out_specs=[_row_spec(geom, SSD_INNER), _row_spec(geom, SSD_CONV_DIM),
                   _row_spec(geom, 2 * SSD_HEADS)],
        out_shape=[jax.ShapeDtypeStruct((t, SSD_INNER), BF16),
                   jax.ShapeDtypeStruct((t, SSD_CONV_DIM), BF16),
                   jax.ShapeDtypeStruct((t, 2 * SSD_HEADS), F32)],
        compiler_params=_cparams(("arbitrary",)),
        name="ssd_inproj",
    )(x, modrows, g, wz, wx, wdt)


_CONV_PAD = 8


def _ssd_conv_kernel(x_ref, w_ref, b_ref, o_ref, pad_ref, *, segments, rows):
    half = SSD_CONV // 2
    w = w_ref[...]
    bias = b_ref[...]
    cols = x_ref.shape[2]
    zeros = jnp.zeros((_CONV_PAD, cols), F32)
    for start, length in segments:
        pad_ref[0:_CONV_PAD, :] = zeros
        pad_ref[_CONV_PAD + length:2 * _CONV_PAD + length, :] = zeros
        for r in range(0, length, rows):
            pad_ref[_CONV_PAD + r:_CONV_PAD + r + rows, :] = (
                x_ref[0, start + r:start + r + rows, :].astype(F32))
        for r in range(0, length, rows):
            acc = None
            for k in range(SSD_CONV):
                lo = _CONV_PAD + r + k - half
                term = pad_ref[lo:lo + rows, :] * w[k:k + 1, :]
                acc = term if acc is None else acc + term
            o_ref[0, start + r:start + r + rows, :] = _silu(acc + bias).astype(o_ref.dtype)


def _ssd_conv(geom, xbc, conv_w, conv_b):
    cols = 512
    segments = ((0, geom.n_ctx), (geom.n_ctx, geom.n_seq))
    kern = functools.partial(_ssd_conv_kernel, segments=segments, rows=geom.tm)
    xbc3 = xbc.reshape(geom.batch, geom.nt, SSD_CONV_DIM)
    out = pl.pallas_call(
        kern,
        grid=(geom.batch, SSD_CONV_DIM // cols),
        in_specs=[pl.BlockSpec((1, geom.nt, cols), lambda b, j: (b, 0, j)),
                  pl.BlockSpec((SSD_CONV, cols), lambda b, j: (0, j)),
                  pl.BlockSpec((1, cols), lambda b, j: (0, j))],
        out_specs=pl.BlockSpec((1, geom.nt, cols), lambda b, j: (b, 0, j)),
        out_shape=jax.ShapeDtypeStruct(xbc3.shape, BF16),
        scratch_shapes=[pltpu.VMEM((max(geom.n_ctx, geom.n_seq) + 2 * _CONV_PAD, cols), F32)],
        compiler_params=_cparams(("arbitrary", "arbitrary")),
        name="ssd_conv",
    )(xbc3, conv_w, conv_b.reshape(1, SSD_CONV_DIM))
    return out


def _ssd_scan_kernel(xbc_ref, dt_ref, bias_ref, alog_ref, expand_ref, o_ref, state_ref):
    k = pl.program_id(1)
    j = pl.program_id(2)
    L, H, P, G, N = SSD_CHUNK, SSD_HEADS, SSD_HEADDIM, SSD_GROUPS, SSD_STATE
    hg = H // G

    @pl.when(j == 0)
    def _():
        state_ref[...] = jnp.zeros_like(state_ref)

    fwd = k == 0
    dt_all = dt_ref[0]
    dt_raw = jnp.where(fwd, dt_all[:, :H], dt_all[:, H:])
    bias = jnp.where(fwd, bias_ref[0:1, :], bias_ref[1:2, :])
    alog = jnp.where(fwd, alog_ref[0:1, :], alog_ref[1:2, :])
    v = dt_raw + bias
    dtk = jnp.maximum(v, 0.0) + jnp.log(1.0 + jnp.exp(-jnp.abs(v)))
    a = -jnp.exp(alog) * dtk

    row = lax.broadcasted_iota(jnp.int32, (L, L), 0)
    col = lax.broadcasted_iota(jnp.int32, (L, L), 1)
    ahead = jnp.where(fwd, row - col, col - row)
    incl = ahead >= 0
    tri = incl.astype(BF16)
    tri_t = (ahead <= 0).astype(BF16)
    a_cs = None
    a_cs_t = None
    for p in _split_bf16(a, 3):
        t1 = jnp.dot(tri, p, preferred_element_type=F32)
        t2 = lax.dot_general(p, tri_t, _TN, preferred_element_type=F32)
        a_cs = t1 if a_cs is None else a_cs + t1
        a_cs_t = t2 if a_cs_t is None else a_cs_t + t2

    expand = expand_ref[...]
    dt_x = _dot_split(dtk, expand, 2)
    acs_x = _dot_split(a_cs, expand, 2)
    tot_x = jnp.where(fwd, acs_x[L - 1:L, :], acs_x[0:1, :])

    xs = xbc_ref[0, :, 0:SSD_INNER].astype(F32)
    bm = xbc_ref[0, :, SSD_INNER:SSD_INNER + SSD_BC]
    cm = xbc_ref[0, :, SSD_INNER + SSD_BC:SSD_INNER + 2 * SSD_BC]
    xdt = xs * dt_x
    xdt_b = xdt.astype(BF16)
    z_b = (xdt * jnp.exp(tot_x - acs_x)).astype(BF16)
    decay_in = jnp.exp(acs_x)
    state = state_ref[...]
    state_b = state.astype(BF16)
    lane = lax.broadcasted_iota(jnp.int32, (L, 2 * P), 1)
    first_head = lane < P

    new_state = []
    for g in range(G):
        bg = bm[:, g * N:(g + 1) * N]
        cg = cm[:, g * N:(g + 1) * N]
        cb = lax.dot_general(cg, bg, _NT, preferred_element_type=F32)
        cols = slice(g * hg * P, (g + 1) * hg * P)
        y_off = jnp.dot(cg, state_b[:, cols], preferred_element_type=F32) * decay_in[:, cols]
        for hp in range(hg // 2):
            h0 = g * hg + 2 * hp
            gs = []
            for h in (h0, h0 + 1):
                seg = jnp.exp(jnp.where(incl, a_cs[:, h:h + 1] - a_cs_t[h:h + 1, :], -jnp.inf))
                gs.append((cb * seg).astype(BF16))
            lhs = jnp.concatenate(gs, axis=1)
            xp = xdt_b[:, h0 * P:(h0 + 2) * P]
            zero = jnp.zeros_like(xp)
            rhs = jnp.concatenate([jnp.where(first_head, xp, zero),
                                   jnp.where(first_head, zero, xp)], axis=0)
            y_pair = jnp.dot(lhs, rhs, preferred_element_type=F32)
            lo = 2 * hp * P
            o_ref[0, 0, :, h0 * P:(h0 + 2) * P] = y_pair + y_off[:, lo:lo + 2 * P]
        upd = lax.dot_general(bg, z_b[:, cols], _TN, preferred_element_type=F32)
        new_state.append(state[:, cols] * jnp.exp(tot_x[:, cols]) + upd)
    for g in range(G):
        state_ref[:, g * hg * P:(g + 1) * hg * P] = new_state[g]


def _ssd_scan(geom, xbcc, dt, dt_bias, a_log):
    nch = geom.nt // SSD_CHUNK
    cch = geom.n_ctx // SSD_CHUNK

    def chunk_index(k, j):
        rev = jnp.where(j < cch, cch - 1 - j, nch - 1 - (j - cch))
        return jnp.where(k == 0, j, rev)

    expand = (jnp.arange(SSD_HEADS)[:, None] == (jnp.arange(SSD_INNER)[None, :] // SSD_HEADDIM)).astype(BF16)
    dt3 = dt.reshape(geom.batch, geom.nt, 2 * SSD_HEADS)
    return pl.pallas_call(
        _ssd_scan_kernel,
        grid=(geom.batch, 2, nch),
        in_specs=[pl.BlockSpec((1, SSD_CHUNK, SSD_CONV_DIM), lambda b, k, j: (b, chunk_index(k, j), 0)),
                  pl.BlockSpec((1, SSD_CHUNK, 2 * SSD_HEADS), lambda b, k, j: (b, chunk_index(k, j), 0)),
                  pl.BlockSpec((2, SSD_HEADS), lambda b, k, j: (0, 0)),
                  pl.BlockSpec((2, SSD_HEADS), lambda b, k, j: (0, 0)),
                  pl.BlockSpec((SSD_HEADS, SSD_INNER), lambda b, k, j: (0, 0))],
        out_specs=pl.BlockSpec((1, 1, SSD_CHUNK, SSD_INNER), lambda b, k, j: (k, b, chunk_index(k, j), 0)),
        out_shape=jax.ShapeDtypeStruct((2, geom.batch, geom.nt, SSD_INNER), F32),
        scratch_shapes=[pltpu.VMEM((SSD_STATE, SSD_INNER), F32)],
        compiler_params=_cparams(("arbitrary", "arbitrary", "arbitrary")),
        name="ssd_scan",
    )(xbcc, dt3, dt_bias, a_log, expand)


def _route(h2, rwhi_ref, rwlo_ref, rb_ref, rt_ref, cnt_out_ref, cnt_ref):
    tm = h2.shape[0]
    h_hi, h_lo = _split_bf16(h2, 2)
    rw_hi, rw_lo = rwhi_ref[...], rwlo_ref[...]
    logits = (lax.dot_general(rw_hi, h_hi, _NT, preferred_element_type=F32)
              + lax.dot_general(rw_hi, h_lo, _NT, preferred_element_type=F32)
              + lax.dot_general(rw_lo, h_hi, _NT, preferred_element_type=F32))
    scores = _sigmoid(logits)
    biased = scores + rb_ref[...]
    s = [scores[e:e + 1, :] for e in range(N_EXPERTS)]
    b = [biased[e:e + 1, :] for e in range(N_EXPERTS)]

    def pair_max(v):
        m = v[0] + v[1]
        for lo, hi in zip(PAIR_LO[1:], PAIR_HI[1:]):
            m = jnp.maximum(m, v[lo] + v[hi])
        return m

    gsc = [pair_max(b[4 * g:4 * g + 4]) for g in range(N_GROUPS)]
    gbest = jnp.zeros((1, tm), jnp.int32)
    best = gsc[0]
    for g in range(1, N_GROUPS):
        take = gsc[g] > best
        gbest = jnp.where(take, g, gbest)
        best = jnp.where(take, gsc[g], best)

    def pick(vals, j):
        out = vals[j]
        for g in range(1, N_GROUPS):
            out = jnp.where(gbest == g, vals[4 * g + j], out)
        return out

    vb = [pick(b, j) for j in range(EXPERTS_PER_GROUP)]
    vs = [pick(s, j) for j in range(EXPERTS_PER_GROUP)]

    def argmax4(v):
        idx = jnp.zeros((1, tm), jnp.int32)
        m = v[0]
        for j in range(1, EXPERTS_PER_GROUP):
            take = v[j] > m
            idx = jnp.where(take, j, idx)
            m = jnp.where(take, v[j], m)
        return idx

    i1 = argmax4(vb)
    i2 = argmax4([jnp.where(i1 == j, -jnp.inf, vb[j]) for j in range(EXPERTS_PER_GROUP)])
    s1 = sum(jnp.where(i1 == j, vs[j], 0.0) for j in range(EXPERTS_PER_GROUP))
    s2 = sum(jnp.where(i2 == j, vs[j], 0.0) for j in range(EXPERTS_PER_GROUP))
    w1 = s1 / (s1 + s2)
    w2 = s2 / (s1 + s2)
    lo = jnp.minimum(i1, i2)
    hi = jnp.maximum(i1, i2)
    pair = jnp.where(lo == 0, hi - 1, jnp.where(lo == 1, hi + 1, 5))
    combo = gbest * PAIRS_PER_GROUP + pair
    first_lo = i1 < i2
    w_lo = jnp.where(first_lo, w1, w2)
    w_hi = jnp.where(first_lo, w2, w1)

    @pl.when(pl.program_id(0) == 0)
    def _():
        cnt_ref[...] = jnp.zeros(cnt_ref.shape, F32)

    crow = lax.broadcasted_iota(jnp.int32, (ROUTE_ROWS, tm), 0)
    onehot = (crow == combo).astype(F32)
    src = lax.broadcasted_iota(jnp.int32, (tm, tm), 0)
    dst = lax.broadcasted_iota(jnp.int32, (tm, tm), 1)
    upper = (src <= dst).astype(BF16)
    incl = jnp.dot(onehot.astype(BF16), upper, preferred_element_type=F32)
    cnt = cnt_ref[...]
    base = jnp.concatenate([cnt] * (tm // LANES), axis=1)
    rank = jnp.sum(onehot * (incl + base), axis=0, keepdims=True) - 1.0
    cnt_new = cnt + jnp.sum(onehot, axis=1, keepdims=True)
    cnt_ref[...] = cnt_new
    cnt_out_ref[...] = cnt_new

    rt_ref[0:1, :] = combo.astype(F32)
    rt_ref[1:2, :] = rank
    rt_ref[2:8, :] = jnp.zeros((6, tm), F32)
    irow = lax.broadcasted_iota(jnp.int32, (LANES, tm), 0)
    info = jnp.where(irow == 0, w_lo, jnp.where(irow == 1, w_hi, 0.0))
    return info.T


def _residual_route(x, y, mod, n2g, rwhi_ref, rwlo_ref, rb_ref, xo_ref, h2_ref, rt_ref,
                    cnt_out_ref, cnt_ref):
    d = D_MODEL
    x_new = x + mod[:, 2 * d:3 * d] * y
    xo_ref[...] = x_new
    h2 = _norm_mod(x_new, n2g, mod[:, 3 * d:4 * d], mod[:, 4 * d:5 * d])
    h2_ref[:, 0:d] = h2
    h2_ref[:, d:d + LANES] = _route(h2, rwhi_ref, rwlo_ref, rb_ref, rt_ref, cnt_out_ref, cnt_ref)


def _ssd_out_kernel(yd_ref, xs_ref, z_ref, x_ref, mod_ref, dsk_ref, ng_ref, w_ref, n2g_ref,
                    rwhi_ref, rwlo_ref, rb_ref, xo_ref, h2_ref, rt_ref, cnt_out_ref, cnt_ref):
    y = dsk_ref[...] * xs_ref[...].astype(F32) + yd_ref[0] + yd_ref[1]
    y = y * _silu(z_ref[...].astype(F32))
    yn = (y * _rms_scale(y, SSD_INNER) * ng_ref[...]).astype(BF16)
    out = jnp.dot(yn, w_ref[...], preferred_element_type=F32)
    _residual_route(x_ref[...], out, mod_ref[0], n2g_ref[...], rwhi_ref, rwlo_ref, rb_ref,
                    xo_ref, h2_ref, rt_ref, cnt_out_ref, cnt_ref)


def _route_specs(geom):
    in_specs = [_full_spec((1, D_MODEL)), _full_spec((N_EXPERTS, D_MODEL)),
                _full_spec((N_EXPERTS, D_MODEL)), _full_spec((N_EXPERTS, 1))]
    out_specs = [_row_spec(geom, D_MODEL), _row_spec(geom, MOE_ROW),
                 pl.BlockSpec((8, geom.tm), lambda i: (i, 0)),
                 _full_spec((ROUTE_ROWS, LANES))]
    out_shape = [jax.ShapeDtypeStruct((geom.t, D_MODEL), F32),
                 jax.ShapeDtypeStruct((geom.t, MOE_ROW), F32),
                 jax.ShapeDtypeStruct((geom.n_tiles * 8, geom.tm), F32),
                 jax.ShapeDtypeStruct((ROUTE_ROWS, LANES), F32)]
    scratch = [pltpu.VMEM((ROUTE_ROWS, LANES), F32)]
    return in_specs, out_specs, out_shape, scratch


def _ssd_out(geom, ydir, xbcc, z, x, modrows, dskip_x, norm_g, w_out, n2g, rw_hi, rw_lo, rb):
    r_in, r_out, r_shape, r_scratch = _route_specs(geom)
    tm = geom.tm
    yd2 = ydir.reshape(2, geom.t, SSD_INNER)
    xbcc2 = xbcc.reshape(geom.t, SSD_CONV_DIM)
    return pl.pallas_call(
        _ssd_out_kernel,
        grid=(geom.n_tiles,),
        in_specs=[pl.BlockSpec((2, tm, SSD_INNER), lambda i: (0, i, 0)),
                  pl.BlockSpec((tm, SSD_INNER), lambda i: (i, 0)),
                  _row_spec(geom, SSD_INNER), _row_spec(geom, D_MODEL), _mod_spec(geom),
                  _full_spec((1, SSD_INNER)), _full_spec((1, SSD_INNER)),
                  _full_spec(w_out.shape)] + r_in,
        out_specs=r_out,
        out_shape=r_shape,
        scratch_shapes=r_scratch,
        compiler_params=_cparams(("arbitrary",)),
        name="ssd_out",
    )(yd2, xbcc2, z, x, modrows, dskip_x, norm_g, w_out, n2g, rw_hi, rw_lo, rb)


def _da_inproj_kernel(x_ref, mod_ref, g_ref, wqt_ref, wk_ref, wvt_ref, gqx_ref, gk_ref,
                      cos_ref, sin_ref, cost_ref, sint_ref, gsum_ref, gexp_ref,
                      qt_ref, k_ref, vt_ref):
    d = D_MODEL
    mod = mod_ref[0]
    h = _norm_mod(x_ref[...], g_ref[...], mod[:, 0:d], mod[:, d:2 * d]).astype(BF16)
    tm = h.shape[0]

    vt_ref[0, 0] = lax.dot_general(wvt_ref[...], h, _NT, preferred_element_type=F32).astype(BF16)

    yt = lax.dot_general(wqt_ref[...], h, _NT, preferred_element_type=F32)
    cost = cost_ref[...]
    sint = sint_ref[...]
    gqx = gqx_ref[...]
    for hd in range(DA_HEADS):
        y3 = yt[hd * LANES:(hd + 1) * LANES, :].reshape(2, DA_HEAD_DIM, tm)
        ss = jnp.sum(y3 * y3, axis=1, keepdims=True)
        yn = (y3 * lax.rsqrt(ss * (1.0 / DA_HEAD_DIM) + RMS_EPS)).reshape(LANES, tm) * gqx
        y4 = yn.reshape(4, 2, 16, tm)
        partner = jnp.concatenate([y4[:, 1:2], y4[:, 0:1]], axis=1).reshape(LANES, tm)
        qt_ref[0, 0, hd * LANES:(hd + 1) * LANES, :] = (yn * cost + partner * sint).astype(BF16)

    cos = cos_ref[...]
    sin = sin_ref[...]
    lane = lax.broadcasted_iota(jnp.int32, cos.shape, 1)
    first_half = (lane % 32) < 16
    y = jnp.dot(h, wk_ref[...], preferred_element_type=F32)
    ss = jnp.dot((y * y).astype(BF16), gsum_ref[...], preferred_element_type=F32)
    r = lax.rsqrt(ss * (1.0 / DA_HEAD_DIM) + RMS_EPS)
    rx = _dot_split(r, gexp_ref[...], 2)
    yn = y * rx * gk_ref[...]
    for hd in range(DA_HEADS):
        c = yn[:, hd * LANES:(hd + 1) * LANES]
        partner = jnp.where(first_half, pltpu.roll(c, LANES - 16, 1), pltpu.roll(c, 16, 1))
        k_ref[:, hd * LANES:(hd + 1) * LANES] = (c * cos + partner * sin).astype(k_ref.dtype)


def _da_inproj(geom, x, modrows, g, wqt, wk, wvt, gqx, gk, cos_t, sin_t, gsum, gexp):
    t, tm = geom.t, geom.tm
    tpb = geom.tiles_per_batch
    tab_spec = pl.BlockSpec((tm, LANES), lambda i: (i % tpb, 0))
    tabt_spec = pl.BlockSpec((LANES, tm), lambda i: (0, i % tpb))
    tr_spec = pl.BlockSpec((1, 1, D_MODEL, tm), lambda i: (i // tpb, i % tpb, 0, 0))
    tr_shape = jax.ShapeDtypeStruct((geom.batch, tpb, D_MODEL, tm), BF16)
    return pl.pallas_call(
        _da_inproj_kernel,
        grid=(geom.n_tiles,),
        in_specs=[_row_spec(geom, D_MODEL), _mod_spec(geom), _full_spec((1, D_MODEL)),
                  _full_spec(wqt.shape), _full_spec(wk.shape), _full_spec(wvt.shape),
                  _full_spec((LANES, tm)), _full_spec((1, D_MODEL)), tab_spec, tab_spec,
                  tabt_spec, tabt_spec, _full_spec(gsum.shape), _full_spec(gexp.shape)],
        out_specs=[tr_spec, _row_spec(geom, D_MODEL), tr_spec],
        out_shape=[tr_shape, jax.ShapeDtypeStruct((t, D_MODEL), BF16), tr_shape],
        compiler_params=_cparams(("arbitrary",)),
        name="da_inproj",
    )(x, modrows, g, wqt, wk, wvt, gqx, gk, cos_t, sin_t, cos_t.T, sin_t.T, gsum, gexp)


def _da_attn_kernel(qt_ref, k_ref, vt_ref, lq_ref, lk_ref, sgx_ref, o_ref,
                    s0_ref, s1_ref, acc0_ref, acc1_ref, p_ref, *, ctx_tiles, all_tiles, lam_init):
    qi = pl.program_id(2)
    qt = qt_ref[0, 0]
    tk = vt_ref.shape[3]
    row = lax.broadcasted_iota(jnp.int32, qt.shape, 0)
    zero = jnp.zeros_like(qt)
    qm = (jnp.where(row < DA_HEAD_DIM, qt, zero), jnp.where(row < DA_HEAD_DIM, zero, qt))
    tq = qt.shape[1]
    n_chunks = jnp.where(qi < ctx_tiles, ctx_tiles, all_tiles)
    acc_refs = (acc0_ref, acc1_ref)
    for r in acc_refs:
        r[...] = jnp.zeros(r.shape, F32)

    def scores(j, s_ref):
        off = pl.multiple_of(j * tk, tk)
        kc = k_ref[0, pl.ds(off, tk), :]
        for m in range(2):
            s_ref[m] = jnp.dot(kc, qm[m], preferred_element_type=F32)

    def softmax(s_ref, stats):
        probs, new_stats = [], []
        for m in range(2):
            m_old, l_old = stats[m]
            st = s_ref[m]
            m_new = jnp.maximum(m_old, jnp.max(st, axis=0, keepdims=True))
            alpha = jnp.exp2(m_old - m_new)
            pt = jnp.exp2(st - m_new)
            l_new = alpha * l_old + jnp.sum(pt, axis=0, keepdims=True)
            probs.append((alpha, pt.astype(BF16)))
            new_stats.append((m_new, l_new))
        return probs, tuple(new_stats)

    def attend(j, alphas, pts):
        vtc = vt_ref[0, j]
        for m in range(2):
            acc_refs[m][...] = (alphas[m] * acc_refs[m][...]
                                + jnp.dot(vtc, pts[m], preferred_element_type=F32))

    scores(0, s0_ref)
    p_ref[...] = jnp.zeros(p_ref.shape, BF16)

    def body(i, carry):
        stats, alpha_pend = carry
        a = 2 * i
        scores(a + 1, s1_ref)
        pa, stats = softmax(s0_ref, stats)
        attend(jnp.maximum(a - 1, 0), alpha_pend, (p_ref[0], p_ref[1]))
        scores(a + 2, s0_ref)
        pb, stats = softmax(s1_ref, stats)
        attend(a, (pa[0][0], pa[1][0]), (pa[0][1], pa[1][1]))
        p_ref[0] = pb[0][1]
        p_ref[1] = pb[1][1]
        return stats, (pb[0][0], pb[1][0])

    neg = jnp.full((1, tq), -1e30, F32)
    zl = jnp.zeros((1, tq), F32)
    one = jnp.ones((1, tq), F32)
    stats, alpha_pend = lax.fori_loop(0, (n_chunks - 1) // 2, body,
                                      (((neg, zl), (neg, zl)), (one, one)))
    p_last, stats = softmax(s0_ref, stats)
    attend(jnp.maximum(n_chunks - 2, 0), alpha_pend, (p_ref[0], p_ref[1]))
    attend(n_chunks - 1, (p_last[0][0], p_last[1][0]), (p_last[0][1], p_last[1][1]))

    tdot = jnp.sum(lq_ref[...] * lk_ref[...], axis=-1, keepdims=True)
    e = jnp.exp(tdot)
    lam = e[0:1, :] - e[1:2, :] + lam_init
    ot = acc0_ref[...] / stats[0][1] - lam * (acc1_ref[...] / stats[1][1])
    ms = jnp.sum(ot * ot, axis=0, keepdims=True) * (1.0 / DA_V_DIM)
    ot = ot * lax.rsqrt(ms + RMS_EPS) * sgx_ref[...]
    o_ref[0] = ot.T.astype(o_ref.dtype)


def _da_attn(geom, qt, k, vt, lam_q, lam_k, sub_gx, lam_init):
    tq = geom.tm
    b, nt, tpb = geom.batch, geom.nt, geom.tiles_per_batch
    k3 = k.reshape(b, nt, D_MODEL)
    kern = functools.partial(_da_attn_kernel, ctx_tiles=geom.ctx_tiles, all_tiles=tpb,
                             lam_init=lam_init)
    small = lambda shape: pl.BlockSpec(shape, lambda bi, h, i: (0, 0))
    return pl.pallas_call(
        kern,
        grid=(b, DA_HEADS, tpb),
        in_specs=[pl.BlockSpec((1, 1, LANES, tq), lambda bi, h, i: (bi, i, h, 0)),
                  pl.BlockSpec((1, nt, LANES), lambda bi, h, i: (bi, 0, h)),
                  pl.BlockSpec((1, tpb, LANES, tq), lambda bi, h, i: (bi, 0, h, 0)),
                  small((2, DA_HEAD_DIM)), small((2, DA_HEAD_DIM)), small((DA_V_DIM, tq))],
        out_specs=pl.BlockSpec((1, tq, LANES), lambda bi, h, i: (bi, i, h)),
        out_shape=jax.ShapeDtypeStruct((b, nt, D_MODEL), BF16),
        scratch_shapes=[pltpu.VMEM((2, tq, tq), F32), pltpu.VMEM((2, tq, tq), F32),
                        pltpu.VMEM((DA_V_DIM, tq), F32), pltpu.VMEM((DA_V_DIM, tq), F32),
                        pltpu.VMEM((2, tq, tq), BF16)],
        compiler_params=_cparams(("arbitrary", "arbitrary", "arbitrary")),
        name="da_attn",
    )(qt, k3, vt, lam_q, lam_k, sub_gx)


def _da_out_kernel(o_ref, x_ref, mod_ref, w_ref, n2g_ref, rwhi_ref, rwlo_ref, rb_ref,
                   xo_ref, h2_ref, rt_ref, cnt_out_ref, cnt_ref):
    out = jnp.dot(o_ref[...], w_ref[...], preferred_element_type=F32)
    _residual_route(x_ref[...], out, mod_ref[0], n2g_ref[...], rwhi_ref, rwlo_ref, rb_ref,
                    xo_ref, h2_ref, rt_ref, cnt_out_ref, cnt_ref)


def _da_out(geom, o, x, modrows, w_out, n2g, rw_hi, rw_lo, rb):
    r_in, r_out, r_shape, r_scratch = _route_specs(geom)
    return pl.pallas_call(
        _da_out_kernel,
        grid=(geom.n_tiles,),
        in_specs=[_row_spec(geom, D_MODEL), _row_spec(geom, D_MODEL), _mod_spec(geom),
                  _full_spec(w_out.shape)] + r_in,
        out_specs=r_out,
        out_shape=r_shape,
        scratch_shapes=r_scratch,
        compiler_params=_cparams(("arbitrary",)),
        name="da_out",
    )(o.reshape(geom.t, D_MODEL), x, modrows, w_out, n2g, rw_hi, rw_lo, rb)


def _gather_rows(idx_ref, base, src_hbm, dst, sem, rows):
    def body(r, carry):
        t = idx_ref[base + r]
        pltpu.make_async_copy(src_hbm.at[pl.ds(t, 1), :], dst.at[pl.ds(r, 1), :], sem).start()
        return carry
    lax.fori_loop(0, rows, body, 0, unroll=8)


def _wait_rows(src_hbm, dst, sem, rows):
    pltpu.make_async_copy(src_hbm.at[pl.ds(0, rows), :], dst, sem).wait()


def _issue_rows(idx_ref, base, src_hbm, dst, sem, rows):
    for r in range(rows):
        t = idx_ref[base + r]
        pltpu.make_async_copy(src_hbm.at[pl.ds(t, 1), :], dst.at[pl.ds(r, 1), :], sem).start()


def _moe_kernel(e0_ref, e1_ref, used_ref, tok_ref, h2_hbm,
                wg0_ref, wu0_ref, wd0_ref, wg1_ref, wu1_ref, wd1_ref, y_ref, xa, xb, sem):
    del e0_ref, e1_ref
    i = pl.program_id(0)
    tm = y_ref.shape[0]
    n_used = used_ref[0]
    even = i % 2 == 0

    @pl.when(i == 0)
    def _():
        _gather_rows(tok_ref, 0, h2_hbm, xa, sem.at[0], tm)

    def step(cur, cur_sem, nxt, nxt_sem):
        _wait_rows(h2_hbm, cur, cur_sem, tm)
        _issue_rows(tok_ref, (i + 1) * tm, h2_hbm, nxt, nxt_sem, tm)
        x = cur[:, 0:D_MODEL].astype(BF16)
        y = None
        for e, (wg, wu, wd) in enumerate(((wg0_ref, wu0_ref, wd0_ref), (wg1_ref, wu1_ref, wd1_ref))):
            gate = jnp.dot(x, wg[0], preferred_element_type=F32)
            up = jnp.dot(x, wu[0], preferred_element_type=F32)
            hid = (_silu(gate) * up).astype(BF16)
            ye = (jnp.dot(hid, wd[0], preferred_element_type=F32)
                  * cur[:, D_MODEL + e:D_MODEL + e + 1])
            y = ye if y is None else y + ye
        y_ref[...] = y

    @pl.when(jnp.logical_and(i < n_used, even))
    def _():
        step(xa, sem.at[0], xb, sem.at[1])

    @pl.when(jnp.logical_and(i < n_used, jnp.logical_not(even)))
    def _():
        step(xb, sem.at[1], xa, sem.at[0])

    @pl.when(i >= n_used)
    def _():
        y_ref[...] = jnp.zeros_like(y_ref)

    @pl.when(jnp.logical_and(i == n_used, even))
    def _():
        _wait_rows(h2_hbm, xa, sem.at[0], tm)

    @pl.when(jnp.logical_and(i == n_used, jnp.logical_not(even)))
    def _():
        _wait_rows(h2_hbm, xb, sem.at[1], tm)


def _moe_plan(geom, route, counts_rep):
    tm, t = geom.tm, geom.t
    rt = route.reshape(geom.n_tiles, 8, tm)
    combo = rt[:, 0, :].reshape(t).astype(jnp.int32)
    rank = rt[:, 1, :].reshape(t).astype(jnp.int32)
    counts = counts_rep[:N_COMBOS, 0].astype(jnp.int32)
    padded = ((counts + tm - 1) // tm) * tm
    ends = jnp.cumsum(padded)
    offs = ends - padded
    onehot = combo[:, None] == jnp.arange(N_COMBOS, dtype=jnp.int32)[None, :]
    pos = jnp.sum(jnp.where(onehot, offs[None, :], 0), axis=1) + rank
    p_rows = t + N_COMBOS * tm
    n_ptiles = p_rows // tm
    tok_sorted = jnp.zeros((p_rows,), jnp.int32).at[pos].set(jnp.arange(t, dtype=jnp.int32))
    n_used = (ends[-1] // tm).astype(jnp.int32)
    tile_start = jnp.arange(n_ptiles, dtype=jnp.int32) * tm
    last_start = jnp.maximum(n_used - 1, 0) * tm
    tile_combo = jnp.searchsorted(ends, jnp.minimum(tile_start, last_start), side="right")
    tile_combo = jnp.minimum(tile_combo, N_COMBOS - 1).astype(jnp.int32)
    grp = tile_combo // PAIRS_PER_GROUP
    pair = tile_combo % PAIRS_PER_GROUP
    e0 = grp * EXPERTS_PER_GROUP + jnp.asarray(PAIR_LO, jnp.int32)[pair]
    e1 = grp * EXPERTS_PER_GROUP + jnp.asarray(PAIR_HI, jnp.int32)[pair]
    return e0, e1, n_used.reshape(1), tok_sorted, pos


def _moe_experts(geom, h2, plan, wg, wu, wd):
    e0, e1, n_used, tok_sorted, _ = plan
    tm = geom.tm
    p_rows = tok_sorted.shape[0]
    de = wg.shape[2]

    def wspec(shape, which):
        if which == 0:
            return pl.BlockSpec(shape, lambda i, e0r, e1r, ur, tr: (e0r[i], 0, 0))
        return pl.BlockSpec(shape, lambda i, e0r, e1r, ur, tr: (e1r[i], 0, 0))

    grid_spec = pltpu.PrefetchScalarGridSpec(
        num_scalar_prefetch=4,
        grid=(p_rows // tm,),
        in_specs=[pl.BlockSpec(memory_space=pl.ANY),
                  wspec((1, D_MODEL, de), 0), wspec((1, D_MODEL, de), 0), wspec((1, de, D_MODEL), 0),
                  wspec((1, D_MODEL, de), 1), wspec((1, D_MODEL, de), 1), wspec((1, de, D_MODEL), 1)],
        out_specs=pl.BlockSpec((tm, D_MODEL), lambda i, *_: (i, 0)),
        scratch_shapes=[pltpu.VMEM((tm, MOE_ROW), F32), pltpu.VMEM((tm, MOE_ROW), F32),
                        pltpu.SemaphoreType.DMA((2,))],
    )
    return pl.pallas_call(
        _moe_kernel,
        grid_spec=grid_spec,
        out_shape=jax.ShapeDtypeStruct((p_rows, D_MODEL), F32),
        compiler_params=_cparams(("arbitrary",)),
        name="moe_experts",
    )(e0, e1, n_used, tok_sorted, h2, wg, wu, wd, wg, wu, wd)


def _moe_combine_kernel(pos_ref, x_ref, mod_ref, y_hbm, o_ref, ybuf, sem):
    i = pl.program_id(0)
    n = pl.num_programs(0)
    tm = o_ref.shape[0]
    slot = i % 2
    d = D_MODEL

    @pl.when(i == 0)
    def _():
        _gather_rows(pos_ref, 0, y_hbm, ybuf.at[0], sem.at[0], tm)

    @pl.when(i + 1 < n)
    def _():
        _gather_rows(pos_ref, (i + 1) * tm, y_hbm, ybuf.at[1 - slot], sem.at[1 - slot], tm)

    _wait_rows(y_hbm, ybuf.at[slot], sem.at[slot], tm)
    mod = mod_ref[0]
    o_ref[...] = x_ref[...] + mod[:, 5 * d:6 * d] * ybuf[slot]


def _moe_combine(geom, x, modrows, y_sorted, pos):
    tm = geom.tm
    grid_spec = pltpu.PrefetchScalarGridSpec(
        num_scalar_prefetch=1,
        grid=(geom.n_tiles,),
        in_specs=[pl.BlockSpec((tm, D_MODEL), lambda i, p: (i, 0)),
                  pl.BlockSpec((1, 1, 6 * D_MODEL), lambda i, p: (geom.mod_index(i), 0, 0)),
                  pl.BlockSpec(memory_space=pl.ANY)],
        out_specs=pl.BlockSpec((tm, D_MODEL), lambda i, p: (i, 0)),
        scratch_shapes=[pltpu.VMEM((2, tm, D_MODEL), F32), pltpu.SemaphoreType.DMA((2,))],
    )
    return pl.pallas_call(
        _moe_combine_kernel,
        grid_spec=grid_spec,
        out_shape=jax.ShapeDtypeStruct((geom.t, D_MODEL), F32),
        compiler_params=_cparams(("arbitrary",)),
        name="moe_combine",
    )(pos, x, modrows, y_sorted)


def _rope_tables(geom):
    tpos = jnp.arange(geom.n_seq, dtype=jnp.int32)
    pos = jnp.stack([tpos // GRID_W, tpos % GRID_W], axis=-1).astype(F32)
    n_freq = DA_HEAD_DIM // 4
    inv_freq = ROPE_BASE ** (-jnp.arange(n_freq, dtype=F32) / n_freq)
    ang = pos[..., None] * inv_freq
    cos, sin = jnp.cos(ang), jnp.sin(ang)
    cos_l = jnp.broadcast_to(cos[:, None, :, None, :], (geom.n_seq, 2, 2, 2, n_freq))
    sgn = jnp.asarray([-1.0, 1.0], F32)[None, None, None, :, None]
    sin_l = jnp.broadcast_to(sin[:, None, :, None, :], (geom.n_seq, 2, 2, 2, n_freq)) * sgn
    cos_l = cos_l.reshape(geom.n_seq, LANES)
    sin_l = sin_l.reshape(geom.n_seq, LANES)
    cos_t = jnp.concatenate([jnp.ones((geom.n_ctx, LANES), F32), cos_l], axis=0)
    sin_t = jnp.concatenate([jnp.zeros((geom.n_ctx, LANES), F32), sin_l], axis=0)
    return cos_t, sin_t


def _head_major(w):
    return w.reshape(D_MODEL, 2, DA_HEADS, DA_HEAD_DIM).transpose(0, 2, 1, 3).reshape(D_MODEL, D_MODEL)


def kernel(x, c, ctx, c_ctx, ada_w, ada_b, norm1_g, norm2_g, ssd_w_in, ssd_conv_w, ssd_conv_b,
           ssd_dt_bias, ssd_a_log, ssd_d, ssd_norm_g, ssd_w_out, da_w_in, da_q_norm, da_k_norm,
           da_lam_q, da_lam_k, da_sub_norm, da_w_out, router_w, router_b, moe_w_gate, moe_w_up,
           moe_w_down):
    batch, n_seq, d = x.shape
    n_ctx = ctx.shape[1]
    depth = ada_w.shape[0]
    assert d == D_MODEL and batch + 1 <= MOD_ROWS
    geom = _Geom(batch, n_ctx, n_seq)

    cvecs = jnp.zeros((MOD_ROWS, d), F32).at[:batch].set(c).at[batch].set(c_ctx)
    mod_all = _ada_mod(cvecs, ada_w, ada_b)

    xs = jnp.concatenate([ctx, x], axis=1).reshape(geom.t, d)

    rw_t = router_w.T
    rw_hi = rw_t.astype(BF16)
    rw_lo = (rw_t - rw_hi.astype(F32)).astype(BF16)
    rb = router_b.reshape(N_EXPERTS, 1)

    cos_t, sin_t = _rope_tables(geom)
    grp_of_col = jnp.arange(D_MODEL) // DA_HEAD_DIM
    gsum = (grp_of_col[:, None] == jnp.arange(LANES)[None, :]).astype(BF16)
    gexp = gsum.T

    for i in range(depth):
        mod_i = mod_all[i]
        modrows = jnp.stack([jnp.broadcast_to(mod_i[batch], (batch, 6 * d)), mod_i[:batch]],
                            axis=1).reshape(2 * batch, 1, 6 * d)
        g1 = norm1_g[i].reshape(1, d)
        g2 = norm2_g[i].reshape(1, d)
        j = i // 2
        if i % 2 == 0:
            w_in = ssd_w_in[j].astype(BF16)
            wz = w_in[:, :SSD_INNER]
            wx = w_in[:, SSD_INNER:SSD_INNER + SSD_CONV_DIM]
            wdt = w_in[:, SSD_INNER + SSD_CONV_DIM:]
            z, xbc, dt = _ssd_inproj(geom, xs, modrows, g1, wz, wx, wdt)
            xbcc = _ssd_conv(geom, xbc, ssd_conv_w[j], ssd_conv_b[j])
            ydir = _ssd_scan(geom, xbcc, dt, ssd_dt_bias[j], ssd_a_log[j])
            dskip_x = jnp.repeat(ssd_d[j], SSD_HEADDIM).reshape(1, SSD_INNER)
            xs, h2, route, counts = _ssd_out(geom, ydir, xbcc, z, xs, modrows, dskip_x,
                                     ssd_norm_g[j].reshape(1, SSD_INNER),
                                     ssd_w_out[j].astype(BF16), g2, rw_hi, rw_lo, rb)
        else:
            lam_init = 0.8 - 0.6 * math.exp(-0.3 * i)
            w_in = da_w_in[j]
            wqt = _head_major(w_in[:, :D_MODEL]).T.astype(BF16)
            wk = _head_major(w_in[:, D_MODEL:2 * D_MODEL]).astype(BF16)
            wvt = w_in[:, 2 * D_MODEL:].T.astype(BF16)
            q_scale = DA_HEAD_DIM ** -0.5 * LOG2E
            gqx = jnp.broadcast_to((da_q_norm[j].reshape(LANES) * q_scale)[:, None], (LANES, geom.tm))
            gk = jnp.tile(da_k_norm[j].reshape(2 * DA_HEAD_DIM), DA_HEADS).reshape(1, d)
            qt, k, vt = _da_inproj(geom, xs, modrows, g1, wqt, wk, wvt, gqx, gk, cos_t, sin_t,
                                   gsum, gexp)
            sub_gx = jnp.broadcast_to((da_sub_norm[j] * (1.0 - lam_init))[:, None],
                                      (DA_V_DIM, geom.tm))
            o = _da_attn(geom, qt, k, vt, da_lam_q[j], da_lam_k[j], sub_gx, lam_init)
            xs, h2, route, counts = _da_out(geom, o, xs, modrows, da_w_out[j].astype(BF16), g2,
                                    rw_hi, rw_lo, rb)
        plan = _moe_plan(geom, route, counts)
        y_sorted = _moe_experts(geom, h2, plan, moe_w_gate[i].astype(BF16),
                                moe_w_up[i].astype(BF16), moe_w_down[i].astype(BF16))
        xs = _moe_combine(geom, xs, modrows, y_sorted, plan[4])

    return xs.reshape(batch, geom.nt, d)[:, n_ctx:, :]
```

```python
import functools
import math

import jax
import jax.numpy as jnp
from jax import lax
from jax.experimental import pallas as pl
from jax.experimental.pallas import tpu as pltpu

F32 = jnp.float32
BF16 = jnp.bfloat16

D_MODEL = 1024
DEPTH = 4
GRID_W = 64
RMS_EPS = 1e-6

SSD_INNER = 2048
SSD_HEADDIM = 64
SSD_HEADS = 32
SSD_GROUPS = 4
SSD_STATE = 128
SSD_CONV = 5
SSD_CHUNK = 128
SSD_BC = SSD_GROUPS * SSD_STATE
SSD_CONV_DIM = SSD_INNER + 2 * SSD_BC

DA_HEADS = 8
DA_HEAD_DIM = 64
DA_V_DIM = 128
ROPE_BASE = 10000.0
ACC_ROWS = DA_V_DIM + 16

N_EXPERTS = 16
N_GROUPS = 4
EXPERTS_PER_GROUP = 4
D_EXPERT = 512
PAIRS_PER_GROUP = 6
N_COMBOS = N_GROUPS * PAIRS_PER_GROUP
PAIR_LO = (0, 0, 0, 1, 1, 2)
PAIR_HI = (1, 2, 3, 2, 3, 3)
ROUTE_ROWS = 32
MOE_ROW = D_MODEL + 128

LANES = 128
MOD_ROWS = 16
VMEM_LIMIT = 56 * 1024 * 1024

LOG2E = 1.4426950408889634


def _cparams(sem):
    return pltpu.CompilerParams(dimension_semantics=sem, vmem_limit_bytes=VMEM_LIMIT)


def _split_bf16(a, n):
    parts = []
    r = a
    for _ in range(n):
        p = r.astype(BF16)
        parts.append(p)
        r = r - p.astype(F32)
    return parts


def _dot_split(a, b_bf16, n, dims=(((1,), (0,)), ((), ()))):
    out = None
    for p in _split_bf16(a, n):
        t = lax.dot_general(p, b_bf16, dims, preferred_element_type=F32)
        out = t if out is None else out + t
    return out


_NT = (((1,), (1,)), ((), ()))
_TN = (((0,), (0,)), ((), ()))


def _sigmoid(x):
    return 1.0 / (1.0 + jnp.exp(-x))


def _silu(x):
    return x * _sigmoid(x)


def _rms_scale(x, n):
    return lax.rsqrt(jnp.sum(x * x, axis=-1, keepdims=True) * (1.0 / n) + RMS_EPS)


def _norm_mod(x, g, shift, scale):
    y = x * _rms_scale(x, x.shape[-1]) * g
    return y * (1.0 + scale) + shift


def _ada_kernel(c_ref, w_ref, b_ref, o_ref):
    s = _silu(c_ref[...])
    acc = jnp.dot(s, w_ref[0], precision=lax.Precision.HIGHEST, preferred_element_type=F32)
    o_ref[0] = acc + b_ref[0]


def _ada_mod(cvecs, ada_w, ada_b):
    depth, d, n6 = ada_w.shape
    tn = 1536
    return pl.pallas_call(
        _ada_kernel,
        grid=(depth, n6 // tn),
        in_specs=[pl.BlockSpec((MOD_ROWS, d), lambda i, j: (0, 0)),
                  pl.BlockSpec((1, d, tn), lambda i, j: (i, 0, j)),
                  pl.BlockSpec((1, 1, tn), lambda i, j: (i, 0, j))],
        out_specs=pl.BlockSpec((1, MOD_ROWS, tn), lambda i, j: (i, 0, j)),
        out_shape=jax.ShapeDtypeStruct((depth, MOD_ROWS, n6), F32),
        compiler_params=_cparams(("arbitrary", "arbitrary")),
        name="ada_mod",
    )(cvecs, ada_w, ada_b.reshape(depth, 1, n6))


class _Geom:
    def __init__(self, batch, n_ctx, n_seq):
        self.batch, self.n_ctx, self.n_seq = batch, n_ctx, n_seq
        self.nt = n_ctx + n_seq
        self.t = batch * self.nt
        self.tm = 256 if (n_ctx % 256 == 0 and n_seq % 256 == 0) else 128
        self.tiles_per_batch = self.nt // self.tm
        self.ctx_tiles = n_ctx // self.tm
        self.n_tiles = self.t // self.tm

    def mod_index(self, i):
        b = i // self.tiles_per_batch
        r = i % self.tiles_per_batch
        return 2 * b + (r >= self.ctx_tiles).astype(jnp.int32)


def _mod_spec(geom):
    return pl.BlockSpec((1, 1, 6 * D_MODEL), lambda i: (geom.mod_index(i), 0, 0))


def _row_spec(geom, width):
    return pl.BlockSpec((geom.tm, width), lambda i: (i, 0))


def _full_spec(shape):
    return pl.BlockSpec(shape, lambda i: (0,) * len(shape))


def _chunked_dot_store(h, w_ref, o_ref, chunk=512):
    n = w_ref.shape[1]
    for j in range(0, n, chunk):
        c = min(chunk, n - j)
        o_ref[:, j:j + c] = jnp.dot(h, w_ref[:, j:j + c],
                                    preferred_element_type=F32).astype(o_ref.dtype)


def _ssd_inproj_kernel(x_ref, mod_ref, g_ref, wz_ref, wx_ref, wdt_ref, z_ref, xbc_ref, dt_ref):
    d = D_MODEL
    mod = mod_ref[0]
    h = _norm_mod(x_ref[...], g_ref[...], mod[:, 0:d], mod[:, d:2 * d]).astype(BF16)
    _chunked_dot_store(h, wz_ref, z_ref)
    _chunked_dot_store(h, wx_ref, xbc_ref)
    dt_ref[...] = jnp.dot(h, wdt_ref[...], preferred_element_type=F32)


def _ssd_inproj(geom, x, modrows, g, wz, wx, wdt):
    t = geom.t
    return pl.pallas_call(
        _ssd_inproj_kernel,
        grid=(geom.n_tiles,),
        in_specs=[_row_spec(geom, D_MODEL), _mod_spec(geom), _full_spec((1, D_MODEL)),
                  _full_spec(wz.shape), _full_spec(wx.shape), _full_spec(wdt.shape)],
        out_specs=[_row_spec(geom, SSD_INNER), _row_spec(geom, SSD_CONV_DIM),
                   _row_spec(geom, 2 * SSD_HEADS)],
        out_shape=[jax.ShapeDtypeStruct((t, SSD_INNER), BF16),
                   jax.ShapeDtypeStruct((t, SSD_CONV_DIM), BF16),
                   jax.ShapeDtypeStruct((t, 2 * SSD_HEADS), F32)],
        compiler_params=_cparams(("arbitrary",)),
        name="ssd_inproj",
    )(x, modrows, g, wz, wx, wdt)


_CONV_PAD = 8


def _ssd_conv_kernel(x_ref, w_ref, b_ref, o_ref, pad_ref, *, segments, rows):
    half = SSD_CONV // 2
    w = w_ref[...]
    bias = b_ref[...]
    cols = x_ref.shape[2]
    zeros = jnp.zeros((_CONV_PAD, cols), F32)
    for start, length in segments:
        pad_ref[0:_CONV_PAD, :] = zeros
        pad_ref[_CONV_PAD + length:2 * _CONV_PAD + length, :] = zeros
        for r in range(0, length, rows):
            pad_ref[_CONV_PAD + r:_CONV_PAD + r + rows, :] = (
                x_ref[0, start + r:start + r + rows, :].astype(F32))
        for r in range(0, length, rows):
            acc = None
            for k in range(SSD_CONV):
                lo = _CONV_PAD + r + k - half
                term = pad_ref[lo:lo + rows, :] * w[k:k + 1, :]
                acc = term if acc is None else acc + term
            o_ref[0, start + r:start + r + rows, :] = _silu(acc + bias).astype(o_ref.dtype)


def _ssd_conv(geom, xbc, conv_w, conv_b):
    cols = 512
    segments = ((0, geom.n_ctx), (geom.n_ctx, geom.n_seq))
    kern = functools.partial(_ssd_conv_kernel, segments=segments, rows=geom.tm)
    xbc3 = xbc.reshape(geom.batch, geom.nt, SSD_CONV_DIM)
    out = pl.pallas_call(
        kern,
        grid=(geom.batch, SSD_CONV_DIM // cols),
        in_specs=[pl.BlockSpec((1, geom.nt, cols), lambda b, j: (b, 0, j)),
                  pl.BlockSpec((SSD_CONV, cols), lambda b, j: (0, j)),
                  pl.BlockSpec((1, cols), lambda b, j: (0, j))],
        out_specs=pl.BlockSpec((1, geom.nt, cols), lambda b, j: (b, 0, j)),
        out_shape=jax.ShapeDtypeStruct(xbc3.shape, BF16),
        scratch_shapes=[pltpu.VMEM((max(geom.n_ctx, geom.n_seq) + 2 * _CONV_PAD, cols), F32)],
        compiler_params=_cparams(("arbitrary", "arbitrary")),
        name="ssd_conv",
    )(xbc3, conv_w, conv_b.reshape(1, SSD_CONV_DIM))
    return out


def _ssd_scan_kernel(xbc_ref, dt_ref, bias_ref, alog_ref, expand_ref, o_ref, state_ref):
    k = pl.program_id(1)
    j = pl.program_id(2)
    L, H, P, G, N = SSD_CHUNK, SSD_HEADS, SSD_HEADDIM, SSD_GROUPS, SSD_STATE
    hg = H // G

    @pl.when(j == 0)
    def _():
        state_ref[...] = jnp.zeros_like(state_ref)

    fwd = k == 0
    dt_all = dt_ref[0]
    dt_raw = jnp.where(fwd, dt_all[:, :H], dt_all[:, H:])
    bias = jnp.where(fwd, bias_ref[0:1, :], bias_ref[1:2, :])
    alog = jnp.where(fwd, alog_ref[0:1, :], alog_ref[1:2, :])
    v = dt_raw + bias
    dtk = jnp.maximum(v, 0.0) + jnp.log(1.0 + jnp.exp(-jnp.abs(v)))
    a = -jnp.exp(alog) * dtk

    row = lax.broadcasted_iota(jnp.int32, (L, L), 0)
    col = lax.broadcasted_iota(jnp.int32, (L, L), 1)
    ahead = jnp.where(fwd, row - col, col - row)
    incl = ahead >= 0
    tri = incl.astype(BF16)
    tri_t = (ahead <= 0).astype(BF16)
    a_cs = None
    a_cs_t = None
    for p in _split_bf16(a, 3):
        t1 = jnp.dot(tri, p, preferred_element_type=F32)
        t2 = lax.dot_general(p, tri_t, _TN, preferred_element_type=F32)
        a_cs = t1 if a_cs is None else a_cs + t1
        a_cs_t = t2 if a_cs_t is None else a_cs_t + t2

    expand = expand_ref[...]
    dt_x = _dot_split(dtk, expand, 2)
    acs_x = _dot_split(a_cs, expand, 2)
    tot_x = jnp.where(fwd, acs_x[L - 1:L, :], acs_x[0:1, :])

    xs = xbc_ref[0, :, 0:SSD_INNER].astype(F32)
    bm = xbc_ref[0, :, SSD_INNER:SSD_INNER + SSD_BC]
    cm = xbc_ref[0, :, SSD_INNER + SSD_BC:SSD_INNER + 2 * SSD_BC]
    xdt = xs * dt_x
    xdt_b = xdt.astype(BF16)
    z_b = (xdt * jnp.exp(tot_x - acs_x)).astype(BF16)
    decay_in = jnp.exp(acs_x)
    state = state_ref[...]
    state_b = state.astype(BF16)
    lane = lax.broadcasted_iota(jnp.int32, (L, 2 * P), 1)
    first_head = lane < P

    new_state = []
    for g in range(G):
        bg = bm[:, g * N:(g + 1) * N]
        cg = cm[:, g * N:(g + 1) * N]
        cb = lax.dot_general(cg, bg, _NT, preferred_element_type=F32)
        cols = slice(g * hg * P, (g + 1) * hg * P)
        y_off = jnp.dot(cg, state_b[:, cols], preferred_element_type=F32) * decay_in[:, cols]
        for hp in range(hg // 2):
            h0 = g * hg + 2 * hp
            gs = []
            for h in (h0, h0 + 1):
                seg = jnp.exp(jnp.where(incl, a_cs[:, h:h + 1] - a_cs_t[h:h + 1, :], -jnp.inf))
                gs.append((cb * seg).astype(BF16))
            lhs = jnp.concatenate(gs, axis=1)
            xp = xdt_b[:, h0 * P:(h0 + 2) * P]
            zero = jnp.zeros_like(xp)
            rhs = jnp.concatenate([jnp.where(first_head, xp, zero),
                                   jnp.where(first_head, zero, xp)], axis=0)
            y_pair = jnp.dot(lhs, rhs, preferred_element_type=F32)
            lo = 2 * hp * P
            o_ref[0, 0, :, h0 * P:(h0 + 2) * P] = (y_pair + y_off[:, lo:lo + 2 * P]).astype(o_ref.dtype)
        upd = lax.dot_general(bg, z_b[:, cols], _TN, preferred_element_type=F32)
        new_state.append(state[:, cols] * jnp.exp(tot_x[:, cols]) + upd)
    for g in range(G):
        state_ref[:, g * hg * P:(g + 1) * hg * P] = new_state[g]


def _ssd_scan(geom, xbcc, dt, dt_bias, a_log):
    nch = geom.nt // SSD_CHUNK
    cch = geom.n_ctx // SSD_CHUNK

    def chunk_index(k, j):
        rev = jnp.where(j < cch, cch - 1 - j, nch - 1 - (j - cch))
        return jnp.where(k == 0, j, rev)

    expand = (jnp.arange(SSD_HEADS)[:, None] == (jnp.arange(SSD_INNER)[None, :] // SSD_HEADDIM)).astype(BF16)
    dt3 = dt.reshape(geom.batch, geom.nt, 2 * SSD_HEADS)
    return pl.pallas_call(
        _ssd_scan_kernel,
        grid=(geom.batch, 2, nch),
        in_specs=[pl.BlockSpec((1, SSD_CHUNK, SSD_CONV_DIM), lambda b, k, j: (b, chunk_index(k, j), 0)),
                  pl.BlockSpec((1, SSD_CHUNK, 2 * SSD_HEADS), lambda b, k, j: (b, chunk_index(k, j), 0)),
                  pl.BlockSpec((2, SSD_HEADS), lambda b, k, j: (0, 0)),
                  pl.BlockSpec((2, SSD_HEADS), lambda b, k, j: (0, 0)),
                  pl.BlockSpec((SSD_HEADS, SSD_INNER), lambda b, k, j: (0, 0))],
        out_specs=pl.BlockSpec((1, 1, SSD_CHUNK, SSD_INNER), lambda b, k, j: (k, b, chunk_index(k, j), 0)),
        out_shape=jax.ShapeDtypeStruct((2, geom.batch, geom.nt, SSD_INNER), BF16),
        scratch_shapes=[pltpu.VMEM((SSD_STATE, SSD_INNER), F32)],
        compiler_params=_cparams(("arbitrary", "arbitrary", "arbitrary")),
        name="ssd_scan",
    )(xbcc, dt3, dt_bias, a_log, expand)


def _route(h2, rwhi_ref, rwlo_ref, rb_ref, rt_ref, cnt_out_ref, cnt_ref):
    tm = h2.shape[0]
    h_hi, h_lo = _split_bf16(h2, 2)
    rw_hi, rw_lo = rwhi_ref[...], rwlo_ref[...]
    logits = (lax.dot_general(rw_hi, h_hi, _NT, preferred_element_type=F32)
              + lax.dot_general(rw_hi, h_lo, _NT, preferred_element_type=F32)
              + lax.dot_general(rw_lo, h_hi, _NT, preferred_element_type=F32))
    scores = _sigmoid(logits)
    biased = scores + rb_ref[...]
    s = [scores[e:e + 1, :] for e in range(N_EXPERTS)]
    b = [biased[e:e + 1, :] for e in range(N_EXPERTS)]

    def pair_max(v):
        m = v[0] + v[1]
        for lo, hi in zip(PAIR_LO[1:], PAIR_HI[1:]):
            m = jnp.maximum(m, v[lo] + v[hi])
        return m

    gsc = [pair_max(b[4 * g:4 * g + 4]) for g in range(N_GROUPS)]
    gbest = jnp.zeros((1, tm), jnp.int32)
    best = gsc[0]
    for g in range(1, N_GROUPS):
        take = gsc[g] > best
        gbest = jnp.where(take, g, gbest)
        best = jnp.where(take, gsc[g], best)

    def pick(vals, j):
        out = vals[j]
        for g in range(1, N_GROUPS):
            out = jnp.where(gbest == g, vals[4 * g + j], out)
        return out

    vb = [pick(b, j) for j in range(EXPERTS_PER_GROUP)]
    vs = [pick(s, j) for j in range(EXPERTS_PER_GROUP)]

    def argmax4(v):
        idx = jnp.zeros((1, tm), jnp.int32)
        m = v[0]
        for j in range(1, EXPERTS_PER_GROUP):
            take = v[j] > m
            idx = jnp.where(take, j, idx)
            m = jnp.where(take, v[j], m)
        return idx

    i1 = argmax4(vb)
    i2 = argmax4([jnp.where(i1 == j, -jnp.inf, vb[j]) for j in range(EXPERTS_PER_GROUP)])
    s1 = sum(jnp.where(i1 == j, vs[j], 0.0) for j in range(EXPERTS_PER_GROUP))
    s2 = sum(jnp.where(i2 == j, vs[j], 0.0) for j in range(EXPERTS_PER_GROUP))
    w1 = s1 / (s1 + s2)
    w2 = s2 / (s1 + s2)
    lo = jnp.minimum(i1, i2)
    hi = jnp.maximum(i1, i2)
    pair = jnp.where(lo == 0, hi - 1, jnp.where(lo == 1, hi + 1, 5))
    combo = gbest * PAIRS_PER_GROUP + pair
    first_lo = i1 < i2
    w_lo = jnp.where(first_lo, w1, w2)
    w_hi = jnp.where(first_lo, w2, w1)

    @pl.when(pl.program_id(0) == 0)
    def _():
        cnt_ref[...] = jnp.zeros(cnt_ref.shape, F32)

    crow = lax.broadcasted_iota(jnp.int32, (ROUTE_ROWS, tm), 0)
    onehot = (crow == combo).astype(F32)
    src = lax.broadcasted_iota(jnp.int32, (tm, tm), 0)
    dst = lax.broadcasted_iota(jnp.int32, (tm, tm), 1)
    upper = (src <= dst).astype(BF16)
    incl = jnp.dot(onehot.astype(BF16), upper, preferred_element_type=F32)
    cnt = cnt_ref[...]
    base = jnp.concatenate([cnt] * (tm // LANES), axis=1)
    rank = jnp.sum(onehot * (incl + base), axis=0, keepdims=True) - 1.0
    cnt_new = cnt + jnp.sum(onehot, axis=1, keepdims=True)
    cnt_ref[...] = cnt_new
    cnt_out_ref[...] = cnt_new

    rt_ref[0:1, :] = combo.astype(F32)
    rt_ref[1:2, :] = rank
    rt_ref[2:8, :] = jnp.zeros((6, tm), F32)
    irow = lax.broadcasted_iota(jnp.int32, (LANES, tm), 0)
    info = jnp.where(irow == 0, w_lo, jnp.where(irow == 1, w_hi, 0.0))
    return info.T


def _residual_route(x, y, mod, n2g, rwhi_ref, rwlo_ref, rb_ref, xo_ref, h2_ref, rt_ref,
                    cnt_out_ref, cnt_ref):
    d = D_MODEL
    x_new = x + mod[:, 2 * d:3 * d] * y
    xo_ref[...] = x_new
    h2 = _norm_mod(x_new, n2g, mod[:, 3 * d:4 * d], mod[:, 4 * d:5 * d])
    h2_ref[:, 0:d] = h2
    h2_ref[:, d:d + LANES] = _route(h2, rwhi_ref, rwlo_ref, rb_ref, rt_ref, cnt_out_ref, cnt_ref)


def _ssd_out_kernel(yd_ref, xs_ref, z_ref, x_ref, mod_ref, dsk_ref, ng_ref, w_ref, n2g_ref,
                    rwhi_ref, rwlo_ref, rb_ref, xo_ref, h2_ref, rt_ref, cnt_out_ref, cnt_ref):
    y = dsk_ref[...] * xs_ref[...].astype(F32) + yd_ref[0].astype(F32) + yd_ref[1].astype(F32)
    y = y * _silu(z_ref[...].astype(F32))
    yn = (y * _rms_scale(y, SSD_INNER) * ng_ref[...]).astype(BF16)
    out = jnp.dot(yn, w_ref[...], preferred_element_type=F32)
    _residual_route(x_ref[...], out, mod_ref[0], n2g_ref[...], rwhi_ref, rwlo_ref, rb_ref,
                    xo_ref, h2_ref, rt_ref, cnt_out_ref, cnt_ref)


def _route_specs(geom):
    in_specs = [_full_spec((1, D_MODEL)), _full_spec((N_EXPERTS, D_MODEL)),
                _full_spec((N_EXPERTS, D_MODEL)), _full_spec((N_EXPERTS, 1))]
    out_specs = [_row_spec(geom, D_MODEL), _row_spec(geom, MOE_ROW),
                 pl.BlockSpec((8, geom.tm), lambda i: (i, 0)),
                 _full_spec((ROUTE_ROWS, LANES))]
    out_shape = [jax.ShapeDtypeStruct((geom.t, D_MODEL), F32),
                 jax.ShapeDtypeStruct((geom.t, MOE_ROW), F32),
                 jax.ShapeDtypeStruct((geom.n_tiles * 8, geom.tm), F32),
                 jax.ShapeDtypeStruct((ROUTE_ROWS, LANES), F32)]
    scratch = [pltpu.VMEM((ROUTE_ROWS, LANES), F32)]
    return in_specs, out_specs, out_shape, scratch


def _ssd_out(geom, ydir, xbcc, z, x, modrows, dskip_x, norm_g, w_out, n2g, rw_hi, rw_lo, rb):
    r_in, r_out, r_shape, r_scratch = _route_specs(geom)
    tm = geom.tm
    yd2 = ydir.reshape(2, geom.t, SSD_INNER)
    xbcc2 = xbcc.reshape(geom.t, SSD_CONV_DIM)
    return pl.pallas_call(
        _ssd_out_kernel,
        grid=(geom.n_tiles,),
        in_specs=[pl.BlockSpec((2, tm, SSD_INNER), lambda i: (0, i, 0)),
                  pl.BlockSpec((tm, SSD_INNER), lambda i: (i, 0)),
                  _row_spec(geom, SSD_INNER), _row_spec(geom, D_MODEL), _mod_spec(geom),
                  _full_spec((1, SSD_INNER)), _full_spec((1, SSD_INNER)),
                  _full_spec(w_out.shape)] + r_in,
        out_specs=r_out,
        out_shape=r_shape,
        scratch_shapes=r_scratch,
        compiler_params=_cparams(("arbitrary",)),
        name="ssd_out",
    )(yd2, xbcc2, z, x, modrows, dskip_x, norm_g, w_out, n2g, rw_hi, rw_lo, rb)


def _da_inproj_kernel(x_ref, mod_ref, g_ref, wqt_ref, wk_ref, wvt_ref, gqx_ref, gk_ref,
                      cos_ref, sin_ref, cost_ref, sint_ref, gsum_ref, gexp_ref,
                      qt_ref, k_ref, vt_ref):
    d = D_MODEL
    mod = mod_ref[0]
    h = _norm_mod(x_ref[...], g_ref[...], mod[:, 0:d], mod[:, d:2 * d]).astype(BF16)
    tm = h.shape[0]

    vt_ref[0, 0] = lax.dot_general(wvt_ref[...], h, _NT, preferred_element_type=F32).astype(BF16)

    yt = lax.dot_general(wqt_ref[...], h, _NT, preferred_element_type=F32)
    cost = cost_ref[...]
    sint = sint_ref[...]
    gqx = gqx_ref[...]
    for hd in range(DA_HEADS):
        y3 = yt[hd * LANES:(hd + 1) * LANES, :].reshape(2, DA_HEAD_DIM, tm)
        ss = jnp.sum(y3 * y3, axis=1, keepdims=True)
        yn = (y3 * lax.rsqrt(ss * (1.0 / DA_HEAD_DIM) + RMS_EPS)).reshape(LANES, tm) * gqx
        y4 = yn.reshape(4, 2, 16, tm)
        partner = jnp.concatenate([y4[:, 1:2], y4[:, 0:1]], axis=1).reshape(LANES, tm)
        qt_ref[0, 0, hd * LANES:(hd + 1) * LANES, :] = (yn * cost + partner * sint).astype(BF16)

    cos = cos_ref[...]
    sin = sin_ref[...]
    lane = lax.broadcasted_iota(jnp.int32, cos.shape, 1)
    first_half = (lane % 32) < 16
    y = jnp.dot(h, wk_ref[...], preferred_element_type=F32)
    ss = jnp.dot((y * y).astype(BF16), gsum_ref[...], preferred_element_type=F32)
    r = lax.rsqrt(ss * (1.0 / DA_HEAD_DIM) + RMS_EPS)
    rx = _dot_split(r, gexp_ref[...], 2)
    yn = y * rx * gk_ref[...]
    for hd in range(DA_HEADS):
        c = yn[:, hd * LANES:(hd + 1) * LANES]
        partner = jnp.where(first_half, pltpu.roll(c, LANES - 16, 1), pltpu.roll(c, 16, 1))
        k_ref[:, hd * LANES:(hd + 1) * LANES] = (c * cos + partner * sin).astype(k_ref.dtype)


def _da_inproj(geom, x, modrows, g, wqt, wk, wvt, gqx, gk, cos_t, sin_t, gsum, gexp):
    t, tm = geom.t, geom.tm
    tpb = geom.tiles_per_batch
    tab_spec = pl.BlockSpec((tm, LANES), lambda i: (i % tpb, 0))
    tabt_spec = pl.BlockSpec((LANES, tm), lambda i: (0, i % tpb))
    tr_spec = pl.BlockSpec((1, 1, D_MODEL, tm), lambda i: (i // tpb, i % tpb, 0, 0))
    tr_shape = jax.ShapeDtypeStruct((geom.batch, tpb, D_MODEL, tm), BF16)
    return pl.pallas_call(
        _da_inproj_kernel,
        grid=(geom.n_tiles,),
        in_specs=[_row_spec(geom, D_MODEL), _mod_spec(geom), _full_spec((1, D_MODEL)),
                  _full_spec(wqt.shape), _full_spec(wk.shape), _full_spec(wvt.shape),
                  _full_spec((LANES, tm)), _full_spec((1, D_MODEL)), tab_spec, tab_spec,
                  tabt_spec, tabt_spec, _full_spec(gsum.shape), _full_spec(gexp.shape)],
        out_specs=[tr_spec, _row_spec(geom, D_MODEL), tr_spec],
        out_shape=[tr_shape, jax.ShapeDtypeStruct((t, D_MODEL), BF16), tr_shape],
        compiler_params=_cparams(("arbitrary",)),
        name="da_inproj",
    )(x, modrows, g, wqt, wk, wvt, gqx, gk, cos_t, sin_t, cos_t.T, sin_t.T, gsum, gexp)


def _da_attn_kernel(qt_ref, k_ref, vt_ref, lq_ref, lk_ref, sgx_ref, o_ref,
                    s0_ref, s1_ref, acc0_ref, acc1_ref, p_ref, *, ctx_tiles, all_tiles, lam_init,
                    unroll):
    qi = pl.program_id(2)
    qt = qt_ref[0, 0]
    tk = vt_ref.shape[3]
    row = lax.broadcasted_iota(jnp.int32, qt.shape, 0)
    zero = jnp.zeros_like(qt)
    qm = (jnp.where(row < DA_HEAD_DIM, qt, zero), jnp.where(row < DA_HEAD_DIM, zero, qt))
    tq = qt.shape[1]
    n_chunks = jnp.where(qi < ctx_tiles, ctx_tiles, all_tiles)
    acc_refs = (acc0_ref, acc1_ref)
    for r in acc_refs:
        r[...] = jnp.zeros(r.shape, F32)

    def scores(j, s_ref):
        off = pl.multiple_of(j * tk, tk)
        kc = k_ref[0, pl.ds(off, tk), :]
        for m in range(2):
            s_ref[m] = jnp.dot(kc, qm[m], preferred_element_type=F32).astype(BF16)

    def softmax(s_ref, stats):
        probs, new_stats = [], []
        for m in range(2):
            m_old = stats[m]
            sb = s_ref[m]
            m_new = jnp.maximum(m_old, jnp.max(sb, axis=0, keepdims=True).astype(F32))
            alpha = jnp.exp2(m_old - m_new)
            probs.append((alpha, jnp.exp2(sb - m_new.astype(BF16))))
            new_stats.append(m_new)
        return probs, tuple(new_stats)

    ones_rows = jnp.ones((ACC_ROWS - DA_V_DIM, tk), BF16)

    def attend(j, alphas, pts):
        vtc = jnp.concatenate([vt_ref[0, j], ones_rows], axis=0)
        for m in range(2):
            acc_refs[m][...] = (alphas[m] * acc_refs[m][...]
                                + jnp.dot(vtc, pts[m], preferred_element_type=F32))

    scores(0, s0_ref)
    p_ref[...] = jnp.zeros(p_ref.shape, BF16)
    s_refs = (s0_ref, s1_ref)

    def body(i, carry):
        stats, alpha_pend = carry
        a = unroll * i
        prev_j, prev_alpha, prev_p = jnp.maximum(a - 1, 0), alpha_pend, (p_ref[0], p_ref[1])
        for u in range(unroll):
            scores(a + u + 1, s_refs[(u + 1) % 2])
            probs, stats = softmax(s_refs[u % 2], stats)
            attend(prev_j, prev_alpha, prev_p)
            prev_j = a + u
            prev_alpha = (probs[0][0], probs[1][0])
            prev_p = (probs[0][1], probs[1][1])
        p_ref[0] = prev_p[0]
        p_ref[1] = prev_p[1]
        return stats, prev_alpha

    neg = jnp.full((1, tq), -1e30, F32)
    one = jnp.ones((1, tq), F32)
    stats, alpha_pend = lax.fori_loop(0, (n_chunks - 1) // unroll, body, ((neg, neg), (one, one)))
    p_last, stats = softmax(s0_ref, stats)
    attend(jnp.maximum(n_chunks - 2, 0), alpha_pend, (p_ref[0], p_ref[1]))
    attend(n_chunks - 1, (p_last[0][0], p_last[1][0]), (p_last[0][1], p_last[1][1]))

    tdot = jnp.sum(lq_ref[...] * lk_ref[...], axis=-1, keepdims=True)
    e = jnp.exp(tdot)
    lam = e[0:1, :] - e[1:2, :] + lam_init
    dv = DA_V_DIM
    ot = (acc0_ref[0:dv, :] / acc0_ref[dv:dv + 1, :]
          - lam * (acc1_ref[0:dv, :] / acc1_ref[dv:dv + 1, :]))
    ms = jnp.sum(ot * ot, axis=0, keepdims=True) * (1.0 / DA_V_DIM)
    ot = ot * lax.rsqrt(ms + RMS_EPS) * sgx_ref[...]
    o_ref[0] = ot.T.astype(o_ref.dtype)


def _da_attn(geom, qt, k, vt, lam_q, lam_k, sub_gx, lam_init):
    tq = geom.tm
    b, nt, tpb = geom.batch, geom.nt, geom.tiles_per_batch
    k3 = k.reshape(b, nt, D_MODEL)
    unroll = 4 if ((tpb - 1) % 4 == 0 and (geom.ctx_tiles - 1) % 4 == 0) else 2
    assert (tpb - 1) % unroll == 0 and (geom.ctx_tiles - 1) % unroll == 0
    kern = functools.partial(_da_attn_kernel, ctx_tiles=geom.ctx_tiles, all_tiles=tpb,
                             lam_init=lam_init, unroll=unroll)
    small = lambda shape: pl.BlockSpec(shape, lambda bi, h, i: (0, 0))
    return pl.pallas_call(
        kern,
        grid=(b, DA_HEADS, tpb),
        in_specs=[pl.BlockSpec((1, 1, LANES, tq), lambda bi, h, i: (bi, i, h, 0)),
                  pl.BlockSpec((1, nt, LANES), lambda bi, h, i: (bi, 0, h)),
                  pl.BlockSpec((1, tpb, LANES, tq), lambda bi, h, i: (bi, 0, h, 0)),
                  small((2, DA_HEAD_DIM)), small((2, DA_HEAD_DIM)), small((DA_V_DIM, tq))],
        out_specs=pl.BlockSpec((1, tq, LANES), lambda bi, h, i: (bi, i, h)),
        out_shape=jax.ShapeDtypeStruct((b, nt, D_MODEL), BF16),
        scratch_shapes=[pltpu.VMEM((2, tq, tq), BF16), pltpu.VMEM((2, tq, tq), BF16),
                        pltpu.VMEM((ACC_ROWS, tq), F32), pltpu.VMEM((ACC_ROWS, tq), F32),
                        pltpu.VMEM((2, tq, tq), BF16)],
        compiler_params=_cparams(("arbitrary", "arbitrary", "arbitrary")),
        name="da_attn",
    )(qt, k3, vt, lam_q, lam_k, sub_gx)


def _da_out_kernel(o_ref, x_ref, mod_ref, w_ref, n2g_ref, rwhi_ref, rwlo_ref, rb_ref,
                   xo_ref, h2_ref, rt_ref, cnt_out_ref, cnt_ref):
    out = jnp.dot(o_ref[...], w_ref[...], preferred_element_type=F32)
    _residual_route(x_ref[...], out, mod_ref[0], n2g_ref[...], rwhi_ref, rwlo_ref, rb_ref,
                    xo_ref, h2_ref, rt_ref, cnt_out_ref, cnt_ref)


def _da_out(geom, o, x, modrows, w_out, n2g, rw_hi, rw_lo, rb):
    r_in, r_out, r_shape, r_scratch = _route_specs(geom)
    return pl.pallas_call(
        _da_out_kernel,
        grid=(geom.n_tiles,),
        in_specs=[_row_spec(geom, D_MODEL), _row_spec(geom, D_MODEL), _mod_spec(geom),
                  _full_spec(w_out.shape)] + r_in,
        out_specs=r_out,
        out_shape=r_shape,
        scratch_shapes=r_scratch,
        compiler_params=_cparams(("arbitrary",)),
        name="da_out",
    )(o.reshape(geom.t, D_MODEL), x, modrows, w_out, n2g, rw_hi, rw_lo, rb)


def _gather_rows(idx_ref, base, src_hbm, dst, sem, rows):
    def body(r, carry):
        t = idx_ref[base + r]
        pltpu.make_async_copy(src_hbm.at[pl.ds(t, 1), :], dst.at[pl.ds(r, 1), :], sem).start()
        return carry
    lax.fori_loop(0, rows, body, 0, unroll=8)


def _wait_rows(src_hbm, dst, sem, rows):
    pltpu.make_async_copy(src_hbm.at[pl.ds(0, rows), :], dst, sem).wait()


def _issue_rows(idx_ref, base, src_hbm, dst, sem, rows):
    for r in range(rows):
        t = idx_ref[base + r]
        pltpu.make_async_copy(src_hbm.at[pl.ds(t, 1), :], dst.at[pl.ds(r, 1), :], sem).start()


def _moe_kernel(e0_ref, e1_ref, used_ref, tok_ref, h2_hbm,
                wg0_ref, wu0_ref, wd0_ref, wg1_ref, wu1_ref, wd1_ref, y_ref, xa, xb, sem):
    del e0_ref, e1_ref
    i = pl.program_id(0)
    tm = y_ref.shape[0]
    n_used = used_ref[0]
    even = i % 2 == 0

    @pl.when(i == 0)
    def _():
        _gather_rows(tok_ref, 0, h2_hbm, xa, sem.at[0], tm)

    def step(cur, cur_sem, nxt, nxt_sem):
        _wait_rows(h2_hbm, cur, cur_sem, tm)
        _issue_rows(tok_ref, (i + 1) * tm, h2_hbm, nxt, nxt_sem, tm)
        x = cur[:, 0:D_MODEL].astype(BF16)
        y = None
        for e, (wg, wu, wd) in enumerate(((wg0_ref, wu0_ref, wd0_ref), (wg1_ref, wu1_ref, wd1_ref))):
            gate = jnp.dot(x, wg[0], preferred_element_type=F32)
            up = jnp.dot(x, wu[0], preferred_element_type=F32)
            hid = (_silu(gate) * up).astype(BF16)
            ye = (jnp.dot(hid, wd[0], preferred_element_type=F32)
                  * cur[:, D_MODEL + e:D_MODEL + e + 1])
            y = ye if y is None else y + ye
        y_ref[...] = y

    @pl.when(jnp.logical_and(i < n_used, even))
    def _():
        step(xa, sem.at[0], xb, sem.at[1])

    @pl.when(jnp.logical_and(i < n_used, jnp.logical_not(even)))
    def _():
        step(xb, sem.at[1], xa, sem.at[0])

    @pl.when(i >= n_used)
    def _():
        y_ref[...] = jnp.zeros_like(y_ref)

    @pl.when(jnp.logical_and(i == n_used, even))
    def _():
        _wait_rows(h2_hbm, xa, sem.at[0], tm)

    @pl.when(jnp.logical_and(i == n_used, jnp.logical_not(even)))
    def _():
        _wait_rows(h2_hbm, xb, sem.at[1], tm)


def _moe_plan(geom, route, counts_rep):
    tm, t = geom.tm, geom.t
    rt = route.reshape(geom.n_tiles, 8, tm)
    combo = rt[:, 0, :].reshape(t).astype(jnp.int32)
    rank = rt[:, 1, :].reshape(t).astype(jnp.int32)
    counts = counts_rep[:N_COMBOS, 0].astype(jnp.int32)
    padded = ((counts + tm - 1) // tm) * tm
    ends = jnp.cumsum(padded)
    offs = ends - padded
    onehot = combo[:, None] == jnp.arange(N_COMBOS, dtype=jnp.int32)[None, :]
    pos = jnp.sum(jnp.where(onehot, offs[None, :], 0), axis=1) + rank
    p_rows = t + N_COMBOS * tm
    n_ptiles = p_rows // tm
    tok_sorted = jnp.zeros((p_rows,), jnp.int32).at[pos].set(jnp.arange(t, dtype=jnp.int32))
    n_used = (ends[-1] // tm).astype(jnp.int32)
    tile_start = jnp.arange(n_ptiles, dtype=jnp.int32) * tm
    last_start = jnp.maximum(n_used - 1, 0) * tm
    tile_combo = jnp.sum(ends[None, :] <= jnp.minimum(tile_start, last_start)[:, None], axis=1)
    tile_combo = jnp.minimum(tile_combo, N_COMBOS - 1).astype(jnp.int32)
    grp = tile_combo // PAIRS_PER_GROUP
    pair = tile_combo % PAIRS_PER_GROUP
    e0 = grp * EXPERTS_PER_GROUP + jnp.asarray(PAIR_LO, jnp.int32)[pair]
    e1 = grp * EXPERTS_PER_GROUP + jnp.asarray(PAIR_HI, jnp.int32)[pair]
    return e0, e1, n_used.reshape(1), tok_sorted, pos


def _moe_experts(geom, h2, plan, wg, wu, wd):
    e0, e1, n_used, tok_sorted, _ = plan
    tm = geom.tm
    p_rows = tok_sorted.shape[0]
    de = wg.shape[2]

    def wspec(shape, which):
        if which == 0:
            return pl.BlockSpec(shape, lambda i, e0r, e1r, ur, tr: (e0r[i], 0, 0))
        return pl.BlockSpec(shape, lambda i, e0r, e1r, ur, tr: (e1r[i], 0, 0))

    grid_spec = pltpu.PrefetchScalarGridSpec(
        num_scalar_prefetch=4,
        grid=(p_rows // tm,),
        in_specs=[pl.BlockSpec(memory_space=pl.ANY),
                  wspec((1, D_MODEL, de), 0), wspec((1, D_MODEL, de), 0), wspec((1, de, D_MODEL), 0),
                  wspec((1, D_MODEL, de), 1), wspec((1, D_MODEL, de), 1), wspec((1, de, D_MODEL), 1)],
        out_specs=pl.BlockSpec((tm, D_MODEL), lambda i, *_: (i, 0)),
        scratch_shapes=[pltpu.VMEM((tm, MOE_ROW), F32), pltpu.VMEM((tm, MOE_ROW), F32),
                        pltpu.SemaphoreType.DMA((2,))],
    )
    return pl.pallas_call(
        _moe_kernel,
        grid_spec=grid_spec,
        out_shape=jax.ShapeDtypeStruct((p_rows, D_MODEL), F32),
        compiler_params=_cparams(("arbitrary",)),
        name="moe_experts",
    )(e0, e1, n_used, tok_sorted, h2, wg, wu, wd, wg, wu, wd)


def _moe_combine_kernel(pos_ref, x_ref, mod_ref, y_hbm, o_ref, ybuf, sem):
    i = pl.program_id(0)
    n = pl.num_programs(0)
    tm = o_ref.shape[0]
    slot = i % 2
    d = D_MODEL

    @pl.when(i == 0)
    def _():
        _gather_rows(pos_ref, 0, y_hbm, ybuf.at[0], sem.at[0], tm)

    @pl.when(i + 1 < n)
    def _():
        _gather_rows(pos_ref, (i + 1) * tm, y_hbm, ybuf.at[1 - slot], sem.at[1 - slot], tm)

    _wait_rows(y_hbm, ybuf.at[slot], sem.at[slot], tm)
    mod = mod_ref[0]
    o_ref[...] = x_ref[...] + mod[:, 5 * d:6 * d] * ybuf[slot]


def _moe_combine(geom, x, modrows, y_sorted, pos, latent_only):
    tm = geom.tm
    tpb, ctx_tiles = geom.tiles_per_batch, geom.ctx_tiles
    lat_tiles = tpb - ctx_tiles

    def out_index(i, p):
        if not latent_only:
            return (i, 0)
        return ((i // tpb) * lat_tiles + jnp.maximum(i % tpb - ctx_tiles, 0), 0)

    out_rows = geom.batch * geom.n_seq if latent_only else geom.t
    grid_spec = pltpu.PrefetchScalarGridSpec(
        num_scalar_prefetch=1,
        grid=(geom.n_tiles,),
        in_specs=[pl.BlockSpec((tm, D_MODEL), lambda i, p: (i, 0)),
                  pl.BlockSpec((1, 1, 6 * D_MODEL), lambda i, p: (geom.mod_index(i), 0, 0)),
                  pl.BlockSpec(memory_space=pl.ANY)],
        out_specs=pl.BlockSpec((tm, D_MODEL), out_index),
        scratch_shapes=[pltpu.VMEM((2, tm, D_MODEL), F32), pltpu.SemaphoreType.DMA((2,))],
    )
    return pl.pallas_call(
        _moe_combine_kernel,
        grid_spec=grid_spec,
        out_shape=jax.ShapeDtypeStruct((out_rows, D_MODEL), F32),
        compiler_params=_cparams(("arbitrary",)),
        name="moe_combine",
    )(pos, x, modrows, y_sorted)


def _rope_tables(geom):
    tpos = jnp.arange(geom.n_seq, dtype=jnp.int32)
    pos = jnp.stack([tpos // GRID_W, tpos % GRID_W], axis=-1).astype(F32)
    n_freq = DA_HEAD_DIM // 4
    inv_freq = ROPE_BASE ** (-jnp.arange(n_freq, dtype=F32) / n_freq)
    ang = pos[..., None] * inv_freq
    cos, sin = jnp.cos(ang), jnp.sin(ang)
    cos_l = jnp.broadcast_to(cos[:, None, :, None, :], (geom.n_seq, 2, 2, 2, n_freq))
    sgn = jnp.asarray([-1.0, 1.0], F32)[None, None, None, :, None]
    sin_l = jnp.broadcast_to(sin[:, None, :, None, :], (geom.n_seq, 2, 2, 2, n_freq)) * sgn
    cos_l = cos_l.reshape(geom.n_seq, LANES)
    sin_l = sin_l.reshape(geom.n_seq, LANES)
    cos_t = jnp.concatenate([jnp.ones((geom.n_ctx, LANES), F32), cos_l], axis=0)
    sin_t = jnp.concatenate([jnp.zeros((geom.n_ctx, LANES), F32), sin_l], axis=0)
    return cos_t, sin_t


def _head_major(w):
    return w.reshape(D_MODEL, 2, DA_HEADS, DA_HEAD_DIM).transpose(0, 2, 1, 3).reshape(D_MODEL, D_MODEL)


def kernel(x, c, ctx, c_ctx, ada_w, ada_b, norm1_g, norm2_g, ssd_w_in, ssd_conv_w, ssd_conv_b,
           ssd_dt_bias, ssd_a_log, ssd_d, ssd_norm_g, ssd_w_out, da_w_in, da_q_norm, da_k_norm,
           da_lam_q, da_lam_k, da_sub_norm, da_w_out, router_w, router_b, moe_w_gate, moe_w_up,
           moe_w_down):
    batch, n_seq, d = x.shape
    n_ctx = ctx.shape[1]
    depth = ada_w.shape[0]
    assert d == D_MODEL and batch + 1 <= MOD_ROWS
    geom = _Geom(batch, n_ctx, n_seq)

    cvecs = jnp.zeros((MOD_ROWS, d), F32).at[:batch].set(c).at[batch].set(c_ctx)
    mod_all = _ada_mod(cvecs, ada_w, ada_b)

    xs = jnp.concatenate([ctx, x], axis=1).reshape(geom.t, d)

    rw_t = router_w.T
    rw_hi = rw_t.astype(BF16)
    rw_lo = (rw_t - rw_hi.astype(F32)).astype(BF16)
    rb = router_b.reshape(N_EXPERTS, 1)

    cos_t, sin_t = _rope_tables(geom)
    grp_of_col = jnp.arange(D_MODEL) // DA_HEAD_DIM
    gsum = (grp_of_col[:, None] == jnp.arange(LANES)[None, :]).astype(BF16)
    gexp = gsum.T

    for i in range(depth):
        mod_i = mod_all[i]
        modrows = jnp.stack([jnp.broadcast_to(mod_i[batch], (batch, 6 * d)), mod_i[:batch]],
                            axis=1).reshape(2 * batch, 1, 6 * d)
        g1 = norm1_g[i].reshape(1, d)
        g2 = norm2_g[i].reshape(1, d)
        j = i // 2
        if i % 2 == 0:
            w_in = ssd_w_in[j].astype(BF16)
            wz = w_in[:, :SSD_INNER]
            wx = w_in[:, SSD_INNER:SSD_INNER + SSD_CONV_DIM]
            wdt = w_in[:, SSD_INNER + SSD_CONV_DIM:]
            z, xbc, dt = _ssd_inproj(geom, xs, modrows, g1, wz, wx, wdt)
            xbcc = _ssd_conv(geom, xbc, ssd_conv_w[j], ssd_conv_b[j])
            ydir = _ssd_scan(geom, xbcc, dt, ssd_dt_bias[j], ssd_a_log[j])
            dskip_x = jnp.repeat(ssd_d[j], SSD_HEADDIM).reshape(1, SSD_INNER)
            xs, h2, route, counts = _ssd_out(geom, ydir, xbcc, z, xs, modrows, dskip_x,
                                     ssd_norm_g[j].reshape(1, SSD_INNER),
                                     ssd_w_out[j].astype(BF16), g2, rw_hi, rw_lo, rb)
        else:
            lam_init = 0.8 - 0.6 * math.exp(-0.3 * i)
            w_in = da_w_in[j]
            wqt = _head_major(w_in[:, :D_MODEL]).T.astype(BF16)
            wk = _head_major(w_in[:, D_MODEL:2 * D_MODEL]).astype(BF16)
            wvt = w_in[:, 2 * D_MODEL:].T.astype(BF16)
            q_scale = DA_HEAD_DIM ** -0.5 * LOG2E
            gqx = jnp.broadcast_to((da_q_norm[j].reshape(LANES) * q_scale)[:, None], (LANES, geom.tm))
            gk = jnp.tile(da_k_norm[j].reshape(2 * DA_HEAD_DIM), DA_HEADS).reshape(1, d)
            qt, k, vt = _da_inproj(geom, xs, modrows, g1, wqt, wk, wvt, gqx, gk, cos_t, sin_t,
                                   gsum, gexp)
            sub_gx = jnp.broadcast_to((da_sub_norm[j] * (1.0 - lam_init))[:, None],
                                      (DA_V_DIM, geom.tm))
            o = _da_attn(geom, qt, k, vt, da_lam_q[j], da_lam_k[j], sub_gx, lam_init)
            xs, h2, route, counts = _da_out(geom, o, xs, modrows, da_w_out[j].astype(BF16), g2,
                                    rw_hi, rw_lo, rb)
        plan = _moe_plan(geom, route, counts)
        y_sorted = _moe_experts(geom, h2, plan, moe_w_gate[i].astype(BF16),
                                moe_w_up[i].astype(BF16), moe_w_down[i].astype(BF16))
        xs = _moe_combine(geom, xs, modrows, y_sorted, plan[4], latent_only=(i == depth - 1))

    return xs.reshape(batch, n_seq, d)
```

```python
import functools
import math

import jax
import jax.numpy as jnp
from jax import lax
from jax.experimental import pallas as pl
from jax.experimental.pallas import tpu as pltpu

F32 = jnp.float32
BF16 = jnp.bfloat16

D_MODEL = 1024
DEPTH = 4
GRID_W = 64
RMS_EPS = 1e-6

SSD_INNER = 2048
SSD_HEADDIM = 64
SSD_HEADS = 32
SSD_GROUPS = 4
SSD_STATE = 128
SSD_CONV = 5
SSD_CHUNK = 128
SSD_BC = SSD_GROUPS * SSD_STATE
SSD_CONV_DIM = SSD_INNER + 2 * SSD_BC

DA_HEADS = 8
DA_HEAD_DIM = 64
DA_V_DIM = 128
ROPE_BASE = 10000.0
ACC_ROWS = DA_V_DIM + 16

N_EXPERTS = 16
N_GROUPS = 4
EXPERTS_PER_GROUP = 4
D_EXPERT = 512
PAIRS_PER_GROUP = 6
N_COMBOS = N_GROUPS * PAIRS_PER_GROUP
PAIR_LO = (0, 0, 0, 1, 1, 2)
PAIR_HI = (1, 2, 3, 2, 3, 3)
ROUTE_ROWS = 32
MOE_ROW = D_MODEL + 128

LANES = 128
MOD_ROWS = 16
VMEM_LIMIT = 56 * 1024 * 1024

LOG2E = 1.4426950408889634


def _cparams(sem):
    return pltpu.CompilerParams(dimension_semantics=sem, vmem_limit_bytes=VMEM_LIMIT)


def _split_bf16(a, n):
    parts = []
    r = a
    for _ in range(n):
        p = r.astype(BF16)
        parts.append(p)
        r = r - p.astype(F32)
    return parts


def _dot_split(a, b_bf16, n, dims=(((1,), (0,)), ((), ()))):
    out = None
    for p in _split_bf16(a, n):
        t = lax.dot_general(p, b_bf16, dims, preferred_element_type=F32)
        out = t if out is None else out + t
    return out


_NT = (((1,), (1,)), ((), ()))
_TN = (((0,), (0,)), ((), ()))


def _sigmoid(x):
    return 1.0 / (1.0 + jnp.exp(-x))


def _silu(x):
    return x * _sigmoid(x)


def _rms_scale(x, n):
    return lax.rsqrt(jnp.sum(x * x, axis=-1, keepdims=True) * (1.0 / n) + RMS_EPS)


def _norm_mod(x, g, shift, scale):
    y = x * _rms_scale(x, x.shape[-1]) * g
    return y * (1.0 + scale) + shift


def _ada_kernel(c_ref, w_ref, b_ref, o_ref):
    s = _silu(c_ref[...])
    acc = jnp.dot(s, w_ref[0], precision=lax.Precision.HIGHEST, preferred_element_type=F32)
    o_ref[0] = acc + b_ref[0]


def _ada_mod(cvecs, ada_w, ada_b):
    depth, d, n6 = ada_w.shape
    tn = 1536
    return pl.pallas_call(
        _ada_kernel,
        grid=(depth, n6 // tn),
        in_specs=[pl.BlockSpec((MOD_ROWS, d), lambda i, j: (0, 0)),
                  pl.BlockSpec((1, d, tn), lambda i, j: (i, 0, j)),
                  pl.BlockSpec((1, 1, tn), lambda i, j: (i, 0, j))],
        out_specs=pl.BlockSpec((1, MOD_ROWS, tn), lambda i, j: (i, 0, j)),
        out_shape=jax.ShapeDtypeStruct((depth, MOD_ROWS, n6), F32),
        compiler_params=_cparams(("arbitrary", "arbitrary")),
        name="ada_mod",
    )(cvecs, ada_w, ada_b.reshape(depth, 1, n6))


class _Geom:
    def __init__(self, batch, n_ctx, n_seq):
        self.batch, self.n_ctx, self.n_seq = batch, n_ctx, n_seq
        self.nt = n_ctx + n_seq
        self.t = batch * self.nt
        self.tm = 256 if (n_ctx % 256 == 0 and n_seq % 256 == 0) else 128
        self.tiles_per_batch = self.nt // self.tm
        self.ctx_tiles = n_ctx // self.tm
        self.n_tiles = self.t // self.tm

    def mod_index(self, i):
        b = i // self.tiles_per_batch
        r = i % self.tiles_per_batch
        return 2 * b + (r >= self.ctx_tiles).astype(jnp.int32)


def _mod_spec(geom):
    return pl.BlockSpec((1, 1, 6 * D_MODEL), lambda i: (geom.mod_index(i), 0, 0))


def _row_spec(geom, width):
    return pl.BlockSpec((geom.tm, width), lambda i: (i, 0))


def _full_spec(shape):
    return pl.BlockSpec(shape, lambda i: (0,) * len(shape))


def _chunked_dot_store(h, w_ref, o_ref, chunk=512):
    n = w_ref.shape[1]
    for j in range(0, n, chunk):
        c = min(chunk, n - j)
        o_ref[:, j:j + c] = jnp.dot(h, w_ref[:, j:j + c],
                                    preferred_element_type=F32).astype(o_ref.dtype)


def _ssd_inproj_kernel(x_ref, mod_ref, g_ref, wz_ref, wx_ref, wdt_ref, z_ref, xbc_ref, dt_ref):
    d = D_MODEL
    mod = mod_ref[0]
    h = _norm_mod(x_ref[...], g_ref[...], mod[:, 0:d], mod[:, d:2 * d]).astype(BF16)
    _chunked_dot_store(h, wz_ref, z_ref)
    _chunked_dot_store(h, wx_ref, xbc_ref)
    dt_ref[...] = jnp.dot(h, wdt_ref[...], preferred_element_type=F32)


def _ssd_inproj(geom, x, modrows, g, wz, wx, wdt):
    t = geom.t
    return pl.pallas_call(
        _ssd_inproj_kernel,
        grid=(geom.n_tiles,),
        in_specs=[_row_spec(geom, D_MODEL), _mod_spec(geom), _full_spec((1, D_MODEL)),
                  _full_spec(wz.shape), _full_spec(wx.shape), _full_spec(wdt.shape)],
        out_specs=[_row_spec(geom, SSD_INNER), _row_spec(geom, SSD_CONV_DIM),
                   _row_spec(geom, 2 * SSD_HEADS)],
        out_shape=[jax.ShapeDtypeStruct((t, SSD_INNER), BF16),
                   jax.ShapeDtypeStruct((t, SSD_CONV_DIM), BF16),
                   jax.ShapeDtypeStruct((t, 2 * SSD_HEADS), F32)],
        compiler_params=_cparams(("arbitrary",)),
        name="ssd_inproj",
    )(x, modrows, g, wz, wx, wdt)


_CONV_PAD = 8


def _ssd_conv_kernel(x_ref, w_ref, b_ref, o_ref, pad_ref, *, segments, rows):
    half = SSD_CONV // 2
    w = w_ref[...]
    bias = b_ref[...]
    cols = x_ref.shape[2]
    zeros = jnp.zeros((_CONV_PAD, cols), F32)
    for start, length in segments:
        pad_ref[0:_CONV_PAD, :] = zeros
        pad_ref[_CONV_PAD + length:2 * _CONV_PAD + length, :] = zeros
        for r in range(0, length, rows):
            pad_ref[_CONV_PAD + r:_CONV_PAD + r + rows, :] = (
                x_ref[0, start + r:start + r + rows, :].astype(F32))
        for r in range(0, length, rows):
            acc = None
            for k in range(SSD_CONV):
                lo = _CONV_PAD + r + k - half
                term = pad_ref[lo:lo + rows, :] * w[k:k + 1, :]
                acc = term if acc is None else acc + term
            o_ref[0, start + r:start + r + rows, :] = _silu(acc + bias).astype(o_ref.dtype)


def _ssd_conv(geom, xbc, conv_w, conv_b):
    cols = 512
    segments = ((0, geom.n_ctx), (geom.n_ctx, geom.n_seq))
    kern = functools.partial(_ssd_conv_kernel, segments=segments, rows=geom.tm)
    xbc3 = xbc.reshape(geom.batch, geom.nt, SSD_CONV_DIM)
    out = pl.pallas_call(
        kern,
        grid=(geom.batch, SSD_CONV_DIM // cols),
        in_specs=[pl.BlockSpec((1, geom.nt, cols), lambda b, j: (b, 0, j)),
                  pl.BlockSpec((SSD_CONV, cols), lambda b, j: (0, j)),
                  pl.BlockSpec((1, cols), lambda b, j: (0, j))],
        out_specs=pl.BlockSpec((1, geom.nt, cols), lambda b, j: (b, 0, j)),
        out_shape=jax.ShapeDtypeStruct(xbc3.shape, BF16),
        scratch_shapes=[pltpu.VMEM((max(geom.n_ctx, geom.n_seq) + 2 * _CONV_PAD, cols), F32)],
        compiler_params=_cparams(("arbitrary", "arbitrary")),
        name="ssd_conv",
    )(xbc3, conv_w, conv_b.reshape(1, SSD_CONV_DIM))
    return out


def _ssd_scan_kernel(xbc_ref, dt_ref, bias_ref, alog_ref, expand_ref, o_ref, state_ref):
    k = pl.program_id(1)
    j = pl.program_id(2)
    L, H, P, G, N = SSD_CHUNK, SSD_HEADS, SSD_HEADDIM, SSD_GROUPS, SSD_STATE
    hg = H // G

    @pl.when(j == 0)
    def _():
        state_ref[...] = jnp.zeros_like(state_ref)

    fwd = k == 0
    dt_all = dt_ref[0]
    dt_raw = jnp.where(fwd, dt_all[:, :H], dt_all[:, H:])
    bias = jnp.where(fwd, bias_ref[0:1, :], bias_ref[1:2, :])
    alog = jnp.where(fwd, alog_ref[0:1, :], alog_ref[1:2, :])
    v = dt_raw + bias
    dtk = jnp.maximum(v, 0.0) + jnp.log(1.0 + jnp.exp(-jnp.abs(v)))
    a = -jnp.exp(alog) * dtk

    row = lax.broadcasted_iota(jnp.int32, (L, L), 0)
    col = lax.broadcasted_iota(jnp.int32, (L, L), 1)
    ahead = jnp.where(fwd, row - col, col - row)
    incl = ahead >= 0
    tri = incl.astype(BF16)
    tri_t = (ahead <= 0).astype(BF16)
    a_cs = None
    a_cs_t = None
    for p in _split_bf16(a, 3):
        t1 = jnp.dot(tri, p, preferred_element_type=F32)
        t2 = lax.dot_general(p, tri_t, _TN, preferred_element_type=F32)
        a_cs = t1 if a_cs is None else a_cs + t1
        a_cs_t = t2 if a_cs_t is None else a_cs_t + t2

    expand = expand_ref[...]
    dt_x = _dot_split(dtk, expand, 2)
    acs_x = _dot_split(a_cs, expand, 2)
    tot_x = jnp.where(fwd, acs_x[L - 1:L, :], acs_x[0:1, :])

    xs = xbc_ref[0, :, 0:SSD_INNER].astype(F32)
    bm = xbc_ref[0, :, SSD_INNER:SSD_INNER + SSD_BC]
    cm = xbc_ref[0, :, SSD_INNER + SSD_BC:SSD_INNER + 2 * SSD_BC]
    xdt = xs * dt_x
    xdt_b = xdt.astype(BF16)
    z_b = (xdt * jnp.exp(tot_x - acs_x)).astype(BF16)
    decay_in = jnp.exp(acs_x)
    state = state_ref[...]
    state_b = state.astype(BF16)
    lane = lax.broadcasted_iota(jnp.int32, (L, 2 * P), 1)
    first_head = lane < P

    new_state = []
    for g in range(G):
        bg = bm[:, g * N:(g + 1) * N]
        cg = cm[:, g * N:(g + 1) * N]
        cb = lax.dot_general(cg, bg, _NT, preferred_element_type=F32)
        cols = slice(g * hg * P, (g + 1) * hg * P)
        y_off = jnp.dot(cg, state_b[:, cols], preferred_element_type=F32) * decay_in[:, cols]
        for hp in range(hg // 2):
            h0 = g * hg + 2 * hp
            gs = []
            for h in (h0, h0 + 1):
                seg = jnp.exp(jnp.where(incl, a_cs[:, h:h + 1] - a_cs_t[h:h + 1, :], -jnp.inf))
                gs.append((cb * seg).astype(BF16))
            lhs = jnp.concatenate(gs, axis=1)
            xp = xdt_b[:, h0 * P:(h0 + 2) * P]
            zero = jnp.zeros_like(xp)
            rhs = jnp.concatenate([jnp.where(first_head, xp, zero),
                                   jnp.where(first_head, zero, xp)], axis=0)
            y_pair = jnp.dot(lhs, rhs, preferred_element_type=F32)
            lo = 2 * hp * P
            o_ref[0, 0, :, h0 * P:(h0 + 2) * P] = (y_pair + y_off[:, lo:lo + 2 * P]).astype(o_ref.dtype)
        upd = lax.dot_general(bg, z_b[:, cols], _TN, preferred_element_type=F32)
        new_state.append(state[:, cols] * jnp.exp(tot_x[:, cols]) + upd)
    for g in range(G):
        state_ref[:, g * hg * P:(g + 1) * hg * P] = new_state[g]


def _ssd_scan(geom, xbcc, dt, dt_bias, a_log):
    nch = geom.nt // SSD_CHUNK
    cch = geom.n_ctx // SSD_CHUNK

    def chunk_index(k, j):
        rev = jnp.where(j < cch, cch - 1 - j, nch - 1 - (j - cch))
        return jnp.where(k == 0, j, rev)

    expand = (jnp.arange(SSD_HEADS)[:, None] == (jnp.arange(SSD_INNER)[None, :] // SSD_HEADDIM)).astype(BF16)
    dt3 = dt.reshape(geom.batch, geom.nt, 2 * SSD_HEADS)
    return pl.pallas_call(
        _ssd_scan_kernel,
        grid=(geom.batch, 2, nch),
        in_specs=[pl.BlockSpec((1, SSD_CHUNK, SSD_CONV_DIM), lambda b, k, j: (b, chunk_index(k, j), 0)),
                  pl.BlockSpec((1, SSD_CHUNK, 2 * SSD_HEADS), lambda b, k, j: (b, chunk_index(k, j), 0)),
                  pl.BlockSpec((2, SSD_HEADS), lambda b, k, j: (0, 0)),
                  pl.BlockSpec((2, SSD_HEADS), lambda b, k, j: (0, 0)),
                  pl.BlockSpec((SSD_HEADS, SSD_INNER), lambda b, k, j: (0, 0))],
        out_specs=pl.BlockSpec((1, 1, SSD_CHUNK, SSD_INNER), lambda b, k, j: (k, b, chunk_index(k, j), 0)),
        out_shape=jax.ShapeDtypeStruct((2, geom.batch, geom.nt, SSD_INNER), BF16),
        scratch_shapes=[pltpu.VMEM((SSD_STATE, SSD_INNER), F32)],
        compiler_params=_cparams(("arbitrary", "arbitrary", "arbitrary")),
        name="ssd_scan",
    )(xbcc, dt3, dt_bias, a_log, expand)


def _route(h2, rwhi_ref, rwlo_ref, rb_ref, rt_ref, cnt_out_ref, cnt_ref):
    tm = h2.shape[0]
    h_hi, h_lo = _split_bf16(h2, 2)
    rw_hi, rw_lo = rwhi_ref[...], rwlo_ref[...]
    logits = (lax.dot_general(rw_hi, h_hi, _NT, preferred_element_type=F32)
              + lax.dot_general(rw_hi, h_lo, _NT, preferred_element_type=F32)
              + lax.dot_general(rw_lo, h_hi, _NT, preferred_element_type=F32))
    scores = _sigmoid(logits)
    biased = scores + rb_ref[...]
    s = [scores[e:e + 1, :] for e in range(N_EXPERTS)]
    b = [biased[e:e + 1, :] for e in range(N_EXPERTS)]

    def pair_max(v):
        m = v[0] + v[1]
        for lo, hi in zip(PAIR_LO[1:], PAIR_HI[1:]):
            m = jnp.maximum(m, v[lo] + v[hi])
        return m

    gsc = [pair_max(b[4 * g:4 * g + 4]) for g in range(N_GROUPS)]
    gbest = jnp.zeros((1, tm), jnp.int32)
    best = gsc[0]
    for g in range(1, N_GROUPS):
        take = gsc[g] > best
        gbest = jnp.where(take, g, gbest)
        best = jnp.where(take, gsc[g], best)

    def pick(vals, j):
        out = vals[j]
        for g in range(1, N_GROUPS):
            out = jnp.where(gbest == g, vals[4 * g + j], out)
        return out

    vb = [pick(b, j) for j in range(EXPERTS_PER_GROUP)]
    vs = [pick(s, j) for j in range(EXPERTS_PER_GROUP)]

    def argmax4(v):
        idx = jnp.zeros((1, tm), jnp.int32)
        m = v[0]
        for j in range(1, EXPERTS_PER_GROUP):
            take = v[j] > m
            idx = jnp.where(take, j, idx)
            m = jnp.where(take, v[j], m)
        return idx

    i1 = argmax4(vb)
    i2 = argmax4([jnp.where(i1 == j, -jnp.inf, vb[j]) for j in range(EXPERTS_PER_GROUP)])
    s1 = sum(jnp.where(i1 == j, vs[j], 0.0) for j in range(EXPERTS_PER_GROUP))
    s2 = sum(jnp.where(i2 == j, vs[j], 0.0) for j in range(EXPERTS_PER_GROUP))
    w1 = s1 / (s1 + s2)
    w2 = s2 / (s1 + s2)
    lo = jnp.minimum(i1, i2)
    hi = jnp.maximum(i1, i2)
    pair = jnp.where(lo == 0, hi - 1, jnp.where(lo == 1, hi + 1, 5))
    combo = gbest * PAIRS_PER_GROUP + pair
    first_lo = i1 < i2
    w_lo = jnp.where(first_lo, w1, w2)
    w_hi = jnp.where(first_lo, w2, w1)

    @pl.when(pl.program_id(0) == 0)
    def _():
        cnt_ref[...] = jnp.zeros(cnt_ref.shape, F32)

    crow = lax.broadcasted_iota(jnp.int32, (ROUTE_ROWS, tm), 0)
    onehot = (crow == combo).astype(F32)
    src = lax.broadcasted_iota(jnp.int32, (tm, tm), 0)
    dst = lax.broadcasted_iota(jnp.int32, (tm, tm), 1)
    upper = (src <= dst).astype(BF16)
    incl = jnp.dot(onehot.astype(BF16), upper, preferred_element_type=F32)
    cnt = cnt_ref[...]
    base = jnp.concatenate([cnt] * (tm // LANES), axis=1)
    rank = jnp.sum(onehot * (incl + base), axis=0, keepdims=True) - 1.0
    cnt_new = cnt + jnp.sum(onehot, axis=1, keepdims=True)
    cnt_ref[...] = cnt_new
    cnt_out_ref[...] = cnt_new

    rt_ref[0:1, :] = combo.astype(F32)
    rt_ref[1:2, :] = rank
    rt_ref[2:8, :] = jnp.zeros((6, tm), F32)
    irow = lax.broadcasted_iota(jnp.int32, (LANES, tm), 0)
    info = jnp.where(irow == 0, w_lo, jnp.where(irow == 1, w_hi, 0.0))
    return info.T


def _residual_route(x, y, mod, n2g, rwhi_ref, rwlo_ref, rb_ref, xo_ref, h2_ref, rt_ref,
                    cnt_out_ref, cnt_ref):
    d = D_MODEL
    x_new = x + mod[:, 2 * d:3 * d] * y
    xo_ref[...] = x_new
    h2 = _norm_mod(x_new, n2g, mod[:, 3 * d:4 * d], mod[:, 4 * d:5 * d])
    h2_ref[:, 0:d] = h2
    h2_ref[:, d:d + LANES] = _route(h2, rwhi_ref, rwlo_ref, rb_ref, rt_ref, cnt_out_ref, cnt_ref)


def _ssd_out_kernel(yd_ref, xs_ref, z_ref, x_ref, mod_ref, dsk_ref, ng_ref, w_ref, n2g_ref,
                    rwhi_ref, rwlo_ref, rb_ref, xo_ref, h2_ref, rt_ref, cnt_out_ref, cnt_ref):
    y = dsk_ref[...] * xs_ref[...].astype(F32) + yd_ref[0].astype(F32) + yd_ref[1].astype(F32)
    y = y * _silu(z_ref[...].astype(F32))
    yn = (y * _rms_scale(y, SSD_INNER) * ng_ref[...]).astype(BF16)
    out = jnp.dot(yn, w_ref[...], preferred_element_type=F32)
    _residual_route(x_ref[...], out, mod_ref[0], n2g_ref[...], rwhi_ref, rwlo_ref, rb_ref,
                    xo_ref, h2_ref, rt_ref, cnt_out_ref, cnt_ref)


def _route_specs(geom):
    in_specs = [_full_spec((1, D_MODEL)), _full_spec((N_EXPERTS, D_MODEL)),
                _full_spec((N_EXPERTS, D_MODEL)), _full_spec((N_EXPERTS, 1))]
    out_specs = [_row_spec(geom, D_MODEL), _row_spec(geom, MOE_ROW),
                 pl.BlockSpec((8, geom.tm), lambda i: (i, 0)),
                 _full_spec((ROUTE_ROWS, LANES))]
    out_shape = [jax.ShapeDtypeStruct((geom.t, D_MODEL), F32),
                 jax.ShapeDtypeStruct((geom.t, MOE_ROW), F32),
                 jax.ShapeDtypeStruct((geom.n_tiles * 8, geom.tm), F32),
                 jax.ShapeDtypeStruct((ROUTE_ROWS, LANES), F32)]
    scratch = [pltpu.VMEM((ROUTE_ROWS, LANES), F32)]
    return in_specs, out_specs, out_shape, scratch


def _ssd_out(geom, ydir, xbcc, z, x, modrows, dskip_x, norm_g, w_out, n2g, rw_hi, rw_lo, rb):
    r_in, r_out, r_shape, r_scratch = _route_specs(geom)
    tm = geom.tm
    yd2 = ydir.reshape(2, geom.t, SSD_INNER)
    xbcc2 = xbcc.reshape(geom.t, SSD_CONV_DIM)
    return pl.pallas_call(
        _ssd_out_kernel,
        grid=(geom.n_tiles,),
        in_specs=[pl.BlockSpec((2, tm, SSD_INNER), lambda i: (0, i, 0)),
                  pl.BlockSpec((tm, SSD_INNER), lambda i: (i, 0)),
                  _row_spec(geom, SSD_INNER), _row_spec(geom, D_MODEL), _mod_spec(geom),
                  _full_spec((1, SSD_INNER)), _full_spec((1, SSD_INNER)),
                  _full_spec(w_out.shape)] + r_in,
        out_specs=r_out,
        out_shape=r_shape,
        scratch_shapes=r_scratch,
        compiler_params=_cparams(("arbitrary",)),
        name="ssd_out",
    )(yd2, xbcc2, z, x, modrows, dskip_x, norm_g, w_out, n2g, rw_hi, rw_lo, rb)


def _da_inproj_kernel(x_ref, mod_ref, g_ref, wqt_ref, wk_ref, wvt_ref, gqx_ref, gk_ref,
                      cos_ref, sin_ref, cost_ref, sint_ref, gsum_ref, gexp_ref,
                      qt_ref, k_ref, vt_ref):
    d = D_MODEL
    mod = mod_ref[0]
    h = _norm_mod(x_ref[...], g_ref[...], mod[:, 0:d], mod[:, d:2 * d]).astype(BF16)
    tm = h.shape[0]

    vt_ref[0, 0] = lax.dot_general(wvt_ref[...], h, _NT, preferred_element_type=F32).astype(BF16)

    yt = lax.dot_general(wqt_ref[...], h, _NT, preferred_element_type=F32)
    cost = cost_ref[...]
    sint = sint_ref[...]
    gqx = gqx_ref[...]
    for hd in range(DA_HEADS):
        y3 = yt[hd * LANES:(hd + 1) * LANES, :].reshape(2, DA_HEAD_DIM, tm)
        ss = jnp.sum(y3 * y3, axis=1, keepdims=True)
        yn = (y3 * lax.rsqrt(ss * (1.0 / DA_HEAD_DIM) + RMS_EPS)).reshape(LANES, tm) * gqx
        y4 = yn.reshape(4, 2, 16, tm)
        partner = jnp.concatenate([y4[:, 1:2], y4[:, 0:1]], axis=1).reshape(LANES, tm)
        qt_ref[0, 0, hd * LANES:(hd + 1) * LANES, :] = (yn * cost + partner * sint).astype(BF16)

    cos = cos_ref[...]
    sin = sin_ref[...]
    lane = lax.broadcasted_iota(jnp.int32, cos.shape, 1)
    first_half = (lane % 32) < 16
    y = jnp.dot(h, wk_ref[...], preferred_element_type=F32)
    ss = jnp.dot((y * y).astype(BF16), gsum_ref[...], preferred_element_type=F32)
    r = lax.rsqrt(ss * (1.0 / DA_HEAD_DIM) + RMS_EPS)
    rx = _dot_split(r, gexp_ref[...], 2)
    yn = y * rx * gk_ref[...]
    for hd in range(DA_HEADS):
        c = yn[:, hd * LANES:(hd + 1) * LANES]
        partner = jnp.where(first_half, pltpu.roll(c, LANES - 16, 1), pltpu.roll(c, 16, 1))
        k_ref[:, hd * LANES:(hd + 1) * LANES] = (c * cos + partner * sin).astype(k_ref.dtype)


def _da_inproj(geom, x, modrows, g, wqt, wk, wvt, gqx, gk, cos_t, sin_t, gsum, gexp):
    t, tm = geom.t, geom.tm
    tpb = geom.tiles_per_batch
    tab_spec = pl.BlockSpec((tm, LANES), lambda i: (i % tpb, 0))
    tabt_spec = pl.BlockSpec((LANES, tm), lambda i: (0, i % tpb))
    tr_spec = pl.BlockSpec((1, 1, D_MODEL, tm), lambda i: (i // tpb, i % tpb, 0, 0))
    tr_shape = jax.ShapeDtypeStruct((geom.batch, tpb, D_MODEL, tm), BF16)
    return pl.pallas_call(
        _da_inproj_kernel,
        grid=(geom.n_tiles,),
        in_specs=[_row_spec(geom, D_MODEL), _mod_spec(geom), _full_spec((1, D_MODEL)),
                  _full_spec(wqt.shape), _full_spec(wk.shape), _full_spec(wvt.shape),
                  _full_spec((LANES, tm)), _full_spec((1, D_MODEL)), tab_spec, tab_spec,
                  tabt_spec, tabt_spec, _full_spec(gsum.shape), _full_spec(gexp.shape)],
        out_specs=[tr_spec, _row_spec(geom, D_MODEL), tr_spec],
        out_shape=[tr_shape, jax.ShapeDtypeStruct((t, D_MODEL), BF16), tr_shape],
        compiler_params=_cparams(("arbitrary",)),
        name="da_inproj",
    )(x, modrows, g, wqt, wk, wvt, gqx, gk, cos_t, sin_t, cos_t.T, sin_t.T, gsum, gexp)


def _da_attn_kernel(qt_ref, k_ref, vt_ref, lq_ref, lk_ref, sgx_ref, o_ref,
                    s0_ref, s1_ref, acc_ref, p_ref, *, ctx_tiles, all_tiles, lam_init, unroll, heads):
    qi = pl.program_id(2)
    tk = vt_ref.shape[3]
    tq = qt_ref.shape[3]
    n_str = 2 * heads
    row = lax.broadcasted_iota(jnp.int32, (LANES, tq), 0)
    qm = []
    for hh in range(heads):
        qt = qt_ref[0, 0, hh * LANES:(hh + 1) * LANES, :]
        zero = jnp.zeros_like(qt)
        qm += [jnp.where(row < DA_HEAD_DIM, qt, zero), jnp.where(row < DA_HEAD_DIM, zero, qt)]
    n_chunks = jnp.where(qi < ctx_tiles, ctx_tiles, all_tiles)
    acc_ref[...] = jnp.zeros(acc_ref.shape, F32)

    def scores(j, s_ref):
        off = pl.multiple_of(j * tk, tk)
        kc = k_ref[0, pl.ds(off, tk), :]
        for st in range(n_str):
            hh = st // 2
            s_ref[st] = jnp.dot(kc[:, hh * LANES:(hh + 1) * LANES], qm[st],
                                preferred_element_type=F32).astype(BF16)

    def softmax(s_ref, stats):
        alphas, pts, new_stats = [], [], []
        for st in range(n_str):
            m_old = stats[st]
            sb = s_ref[st]
            m_new = jnp.maximum(m_old, jnp.max(sb, axis=0, keepdims=True).astype(F32))
            alphas.append(jnp.exp2(m_old - m_new))
            pts.append(jnp.exp2(sb - m_new.astype(BF16)))
            new_stats.append(m_new)
        return tuple(alphas), tuple(pts), tuple(new_stats)

    ones_rows = jnp.ones((ACC_ROWS - DA_V_DIM, tk), BF16)

    def attend(j, alphas, pts):
        vt_all = vt_ref[0, j]
        for hh in range(heads):
            vtc = jnp.concatenate([vt_all[hh * LANES:(hh + 1) * LANES], ones_rows], axis=0)
            for st in (2 * hh, 2 * hh + 1):
                acc_ref[st] = alphas[st] * acc_ref[st] + jnp.dot(vtc, pts[st],
                                                                 preferred_element_type=F32)

    scores(0, s0_ref)
    p_ref[...] = jnp.zeros(p_ref.shape, BF16)
    s_refs = (s0_ref, s1_ref)

    def body(i, carry):
        stats, alpha_pend = carry
        a = unroll * i
        prev_j, prev_alpha = jnp.maximum(a - 1, 0), alpha_pend
        prev_p = tuple(p_ref[st] for st in range(n_str))
        for u in range(unroll):
            scores(a + u + 1, s_refs[(u + 1) % 2])
            alphas, pts, stats = softmax(s_refs[u % 2], stats)
            attend(prev_j, prev_alpha, prev_p)
            prev_j, prev_alpha, prev_p = a + u, alphas, pts
        for st in range(n_str):
            p_ref[st] = prev_p[st]
        return stats, prev_alpha

    neg = jnp.full((1, tq), -1e30, F32)
    one = jnp.ones((1, tq), F32)
    stats, alpha_pend = lax.fori_loop(0, (n_chunks - 1) // unroll, body,
                                      ((neg,) * n_str, (one,) * n_str))
    a_last, p_last, stats = softmax(s0_ref, stats)
    attend(jnp.maximum(n_chunks - 2, 0), alpha_pend, tuple(p_ref[st] for st in range(n_str)))
    attend(n_chunks - 1, a_last, p_last)

    tdot = jnp.sum(lq_ref[...] * lk_ref[...], axis=-1, keepdims=True)
    e = jnp.exp(tdot)
    lam = e[0:1, :] - e[1:2, :] + lam_init
    dv = DA_V_DIM
    for hh in range(heads):
        a1, a2 = acc_ref[2 * hh], acc_ref[2 * hh + 1]
        ot = a1[0:dv, :] / a1[dv:dv + 1, :] - lam * (a2[0:dv, :] / a2[dv:dv + 1, :])
        ms = jnp.sum(ot * ot, axis=0, keepdims=True) * (1.0 / DA_V_DIM)
        ot = ot * lax.rsqrt(ms + RMS_EPS) * sgx_ref[...]
        o_ref[0, :, hh * LANES:(hh + 1) * LANES] = ot.T.astype(o_ref.dtype)


def _da_attn(geom, qt, k, vt, lam_q, lam_k, sub_gx, lam_init):
    tq = geom.tm
    b, nt, tpb = geom.batch, geom.nt, geom.tiles_per_batch
    k3 = k.reshape(b, nt, D_MODEL)
    heads = 4
    hw = heads * LANES
    unroll = 4 if ((tpb - 1) % 4 == 0 and (geom.ctx_tiles - 1) % 4 == 0) else 2
    assert (tpb - 1) % unroll == 0 and (geom.ctx_tiles - 1) % unroll == 0
    kern = functools.partial(_da_attn_kernel, ctx_tiles=geom.ctx_tiles, all_tiles=tpb,
                             lam_init=lam_init, unroll=unroll, heads=heads)
    small = lambda shape: pl.BlockSpec(shape, lambda bi, h, i: (0, 0))
    return pl.pallas_call(
        kern,
        grid=(b, DA_HEADS // heads, tpb),
        in_specs=[pl.BlockSpec((1, 1, hw, tq), lambda bi, h, i: (bi, i, h, 0)),
                  pl.BlockSpec((1, nt, hw), lambda bi, h, i: (bi, 0, h)),
                  pl.BlockSpec((1, tpb, hw, tq), lambda bi, h, i: (bi, 0, h, 0)),
                  small((2, DA_HEAD_DIM)), small((2, DA_HEAD_DIM)), small((DA_V_DIM, tq))],
        out_specs=pl.BlockSpec((1, tq, hw), lambda bi, h, i: (bi, i, h)),
        out_shape=jax.ShapeDtypeStruct((b, nt, D_MODEL), BF16),
        scratch_shapes=[pltpu.VMEM((2 * heads, tq, tq), BF16), pltpu.VMEM((2 * heads, tq, tq), BF16),
                        pltpu.VMEM((2 * heads, ACC_ROWS, tq), F32),
                        pltpu.VMEM((2 * heads, tq, tq), BF16)],
        compiler_params=_cparams(("arbitrary", "arbitrary", "arbitrary")),
        name="da_attn",
    )(qt, k3, vt, lam_q, lam_k, sub_gx)


def _da_out_kernel(o_ref, x_ref, mod_ref, w_ref, n2g_ref, rwhi_ref, rwlo_ref, rb_ref,
                   xo_ref, h2_ref, rt_ref, cnt_out_ref, cnt_ref):
    out = jnp.dot(o_ref[...], w_ref[...], preferred_element_type=F32)
    _residual_route(x_ref[...], out, mod_ref[0], n2g_ref[...], rwhi_ref, rwlo_ref, rb_ref,
                    xo_ref, h2_ref, rt_ref, cnt_out_ref, cnt_ref)


def _da_out(geom, o, x, modrows, w_out, n2g, rw_hi, rw_lo, rb):
    r_in, r_out, r_shape, r_scratch = _route_specs(geom)
    return pl.pallas_call(
        _da_out_kernel,
        grid=(geom.n_tiles,),
        in_specs=[_row_spec(geom, D_MODEL), _row_spec(geom, D_MODEL), _mod_spec(geom),
                  _full_spec(w_out.shape)] + r_in,
        out_specs=r_out,
        out_shape=r_shape,
        scratch_shapes=r_scratch,
        compiler_params=_cparams(("arbitrary",)),
        name="da_out",
    )(o.reshape(geom.t, D_MODEL), x, modrows, w_out, n2g, rw_hi, rw_lo, rb)


def _gather_rows(idx_ref, base, src_hbm, dst, sem, rows):
    def body(r, carry):
        t = idx_ref[base + r]
        pltpu.make_async_copy(src_hbm.at[pl.ds(t, 1), :], dst.at[pl.ds(r, 1), :], sem).start()
        return carry
    lax.fori_loop(0, rows, body, 0, unroll=8)


def _wait_rows(src_hbm, dst, sem, rows):
    pltpu.make_async_copy(src_hbm.at[pl.ds(0, rows), :], dst, sem).wait()


def _moe_dispatch_kernel(tok_ref, h2_hbm, xs_hbm, sem):
    i = pl.program_id(0)
    n = pl.num_programs(0)
    rows = xs_hbm.shape[0] // n
    base = i * rows

    def body(r, carry):
        t = tok_ref[base + r]
        pltpu.make_async_copy(h2_hbm.at[pl.ds(t, 1), :], xs_hbm.at[pl.ds(base + r, 1), :],
                              sem.at[i % 2]).start()
        return carry
    lax.fori_loop(0, rows, body, 0, unroll=8)

    def drain(step):
        pltpu.make_async_copy(h2_hbm.at[pl.ds(0, rows), :], xs_hbm.at[pl.ds(step * rows, rows), :],
                              sem.at[step % 2]).wait()

    @pl.when(i > 0)
    def _():
        drain(i - 1)

    @pl.when(i == n - 1)
    def _():
        drain(i)


def _moe_dispatch(geom, h2, tok_sorted):
    p_rows = tok_sorted.shape[0]
    grid_spec = pltpu.PrefetchScalarGridSpec(
        num_scalar_prefetch=1,
        grid=(p_rows // geom.tm,),
        in_specs=[pl.BlockSpec(memory_space=pl.ANY)],
        out_specs=pl.BlockSpec(memory_space=pl.ANY),
        scratch_shapes=[pltpu.SemaphoreType.DMA((2,))],
    )
    return pl.pallas_call(
        _moe_dispatch_kernel,
        grid_spec=grid_spec,
        out_shape=jax.ShapeDtypeStruct((p_rows, MOE_ROW), F32),
        compiler_params=_cparams(("arbitrary",)),
        name="moe_dispatch",
    )(tok_sorted, h2)


def _moe_kernel(e0_ref, e1_ref, used_ref, x_ref,
                wg0_ref, wu0_ref, wd0_ref, wg1_ref, wu1_ref, wd1_ref, y_ref):
    del e0_ref, e1_ref
    i = pl.program_id(0)

    @pl.when(i < used_ref[0])
    def _():
        x = x_ref[:, 0:D_MODEL].astype(BF16)
        y = None
        for e, (wg, wu, wd) in enumerate(((wg0_ref, wu0_ref, wd0_ref), (wg1_ref, wu1_ref, wd1_ref))):
            gate = jnp.dot(x, wg[0], preferred_element_type=F32)
            up = jnp.dot(x, wu[0], preferred_element_type=F32)
            hid = (_silu(gate) * up).astype(BF16)
            ye = (jnp.dot(hid, wd[0], preferred_element_type=F32)
                  * x_ref[:, D_MODEL + e:D_MODEL + e + 1])
            y = ye if y is None else y + ye
        y_ref[...] = y

    @pl.when(i >= used_ref[0])
    def _():
        y_ref[...] = jnp.zeros_like(y_ref)


def _moe_plan(geom, route, counts_rep):
    tm, t = geom.tm, geom.t
    rt = route.reshape(geom.n_tiles, 8, tm)
    combo = rt[:, 0, :].reshape(t).astype(jnp.int32)
    rank = rt[:, 1, :].reshape(t).astype(jnp.int32)
    counts = counts_rep[:N_COMBOS, 0].astype(jnp.int32)
    padded = ((counts + tm - 1) // tm) * tm
    ends = jnp.cumsum(padded)
    offs = ends - padded
    onehot = combo[:, None] == jnp.arange(N_COMBOS, dtype=jnp.int32)[None, :]
    pos = jnp.sum(jnp.where(onehot, offs[None, :], 0), axis=1) + rank
    p_rows = t + N_COMBOS * tm
    n_ptiles = p_rows // tm
    tok_sorted = jnp.zeros((p_rows,), jnp.int32).at[pos].set(jnp.arange(t, dtype=jnp.int32))
    n_used = (ends[-1] // tm).astype(jnp.int32)
    tile_start = jnp.arange(n_ptiles, dtype=jnp.int32) * tm
    last_start = jnp.maximum(n_used - 1, 0) * tm
    tile_combo = jnp.sum(ends[None, :] <= jnp.minimum(tile_start, last_start)[:, None], axis=1)
    tile_combo = jnp.minimum(tile_combo, N_COMBOS - 1).astype(jnp.int32)
    grp = tile_combo // PAIRS_PER_GROUP
    pair = tile_combo % PAIRS_PER_GROUP
    e0 = grp * EXPERTS_PER_GROUP + jnp.asarray(PAIR_LO, jnp.int32)[pair]
    e1 = grp * EXPERTS_PER_GROUP + jnp.asarray(PAIR_HI, jnp.int32)[pair]
    return e0, e1, n_used.reshape(1), tok_sorted, pos


def _moe_experts(geom, h2, plan, wg, wu, wd):
    e0, e1, n_used, tok_sorted, _ = plan
    tm = geom.tm
    p_rows = tok_sorted.shape[0]
    de = wg.shape[2]
    xs_sorted = _moe_dispatch(geom, h2, tok_sorted)

    def wspec(shape, which):
        if which == 0:
            return pl.BlockSpec(shape, lambda i, e0r, e1r, ur: (e0r[i], 0, 0))
        return pl.BlockSpec(shape, lambda i, e0r, e1r, ur: (e1r[i], 0, 0))

    grid_spec = pltpu.PrefetchScalarGridSpec(
        num_scalar_prefetch=3,
        grid=(p_rows // tm,),
        in_specs=[pl.BlockSpec((tm, MOE_ROW), lambda i, *_: (i, 0)),
                  wspec((1, D_MODEL, de), 0), wspec((1, D_MODEL, de), 0), wspec((1, de, D_MODEL), 0),
                  wspec((1, D_MODEL, de), 1), wspec((1, D_MODEL, de), 1), wspec((1, de, D_MODEL), 1)],
        out_specs=pl.BlockSpec((tm, D_MODEL), lambda i, *_: (i, 0)),
    )
    return pl.pallas_call(
        _moe_kernel,
        grid_spec=grid_spec,
        out_shape=jax.ShapeDtypeStruct((p_rows, D_MODEL), F32),
        compiler_params=_cparams(("arbitrary",)),
        name="moe_experts",
    )(e0, e1, n_used, xs_sorted, wg, wu, wd, wg, wu, wd)


def _moe_combine_kernel(pos_ref, x_ref, mod_ref, y_hbm, o_ref, ybuf, sem):
    i = pl.program_id(0)
    n = pl.num_programs(0)
    tm = o_ref.shape[0]
    slot = i % 2
    d = D_MODEL

    @pl.when(i == 0)
    def _():
        _gather_rows(pos_ref, 0, y_hbm, ybuf.at[0], sem.at[0], tm)

    @pl.when(i + 1 < n)
    def _():
        _gather_rows(pos_ref, (i + 1) * tm, y_hbm, ybuf.at[1 - slot], sem.at[1 - slot], tm)

    _wait_rows(y_hbm, ybuf.at[slot], sem.at[slot], tm)
    mod = mod_ref[0]
    o_ref[...] = x_ref[...] + mod[:, 5 * d:6 * d] * ybuf[slot]


def _moe_combine(geom, x, modrows, y_sorted, pos, latent_only):
    tm = geom.tm
    tpb, ctx_tiles = geom.tiles_per_batch, geom.ctx_tiles
    lat_tiles = tpb - ctx_tiles

    def out_index(i, p):
        if not latent_only:
            return (i, 0)
        return ((i // tpb) * lat_tiles + jnp.maximum(i % tpb - ctx_tiles, 0), 0)

    out_rows = geom.batch * geom.n_seq if latent_only else geom.t
    grid_spec = pltpu.PrefetchScalarGridSpec(
        num_scalar_prefetch=1,
        grid=(geom.n_tiles,),
        in_specs=[pl.BlockSpec((tm, D_MODEL), lambda i, p: (i, 0)),
                  pl.BlockSpec((1, 1, 6 * D_MODEL), lambda i, p: (geom.mod_index(i), 0, 0)),
                  pl.BlockSpec(memory_space=pl.ANY)],
        out_specs=pl.BlockSpec((tm, D_MODEL), out_index),
        scratch_shapes=[pltpu.VMEM((2, tm, D_MODEL), F32), pltpu.SemaphoreType.DMA((2,))],
    )
    return pl.pallas_call(
        _moe_combine_kernel,
        grid_spec=grid_spec,
        out_shape=jax.ShapeDtypeStruct((out_rows, D_MODEL), F32),
        compiler_params=_cparams(("arbitrary",)),
        name="moe_combine",
    )(pos, x, modrows, y_sorted)


def _rope_tables(geom):
    tpos = jnp.arange(geom.n_seq, dtype=jnp.int32)
    pos = jnp.stack([tpos // GRID_W, tpos % GRID_W], axis=-1).astype(F32)
    n_freq = DA_HEAD_DIM // 4
    inv_freq = ROPE_BASE ** (-jnp.arange(n_freq, dtype=F32) / n_freq)
    ang = pos[..., None] * inv_freq
    cos, sin = jnp.cos(ang), jnp.sin(ang)
    cos_l = jnp.broadcast_to(cos[:, None, :, None, :], (geom.n_seq, 2, 2, 2, n_freq))
    sgn = jnp.asarray([-1.0, 1.0], F32)[None, None, None, :, None]
    sin_l = jnp.broadcast_to(sin[:, None, :, None, :], (geom.n_seq, 2, 2, 2, n_freq)) * sgn
    cos_l = cos_l.reshape(geom.n_seq, LANES)
    sin_l = sin_l.reshape(geom.n_seq, LANES)
    cos_t = jnp.concatenate([jnp.ones((geom.n_ctx, LANES), F32), cos_l], axis=0)
    sin_t = jnp.concatenate([jnp.zeros((geom.n_ctx, LANES), F32), sin_l], axis=0)
    return cos_t, sin_t


def _head_major(w):
    return w.reshape(D_MODEL, 2, DA_HEADS, DA_HEAD_DIM).transpose(0, 2, 1, 3).reshape(D_MODEL, D_MODEL)


def kernel(x, c, ctx, c_ctx, ada_w, ada_b, norm1_g, norm2_g, ssd_w_in, ssd_conv_w, ssd_conv_b,
           ssd_dt_bias, ssd_a_log, ssd_d, ssd_norm_g, ssd_w_out, da_w_in, da_q_norm, da_k_norm,
           da_lam_q, da_lam_k, da_sub_norm, da_w_out, router_w, router_b, moe_w_gate, moe_w_up,
           moe_w_down):
    batch, n_seq, d = x.shape
    n_ctx = ctx.shape[1]
    depth = ada_w.shape[0]
    assert d == D_MODEL and batch + 1 <= MOD_ROWS
    geom = _Geom(batch, n_ctx, n_seq)

    cvecs = jnp.zeros((MOD_ROWS, d), F32).at[:batch].set(c).at[batch].set(c_ctx)
    mod_all = _ada_mod(cvecs, ada_w, ada_b)

    xs = jnp.concatenate([ctx, x], axis=1).reshape(geom.t, d)

    rw_t = router_w.T
    rw_hi = rw_t.astype(BF16)
    rw_lo = (rw_t - rw_hi.astype(F32)).astype(BF16)
    rb = router_b.reshape(N_EXPERTS, 1)

    cos_t, sin_t = _rope_tables(geom)
    grp_of_col = jnp.arange(D_MODEL) // DA_HEAD_DIM
    gsum = (grp_of_col[:, None] == jnp.arange(LANES)[None, :]).astype(BF16)
    gexp = gsum.T

    for i in range(depth):
        mod_i = mod_all[i]
        modrows = jnp.stack([jnp.broadcast_to(mod_i[batch], (batch, 6 * d)), mod_i[:batch]],
                            axis=1).reshape(2 * batch, 1, 6 * d)
        g1 = norm1_g[i].reshape(1, d)
        g2 = norm2_g[i].reshape(1, d)
        j = i // 2
        if i % 2 == 0:
            w_in = ssd_w_in[j].astype(BF16)
            wz = w_in[:, :SSD_INNER]
            wx = w_in[:, SSD_INNER:SSD_INNER + SSD_CONV_DIM]
            wdt = w_in[:, SSD_INNER + SSD_CONV_DIM:]
            z, xbc, dt = _ssd_inproj(geom, xs, modrows, g1, wz, wx, wdt)
            xbcc = _ssd_conv(geom, xbc, ssd_conv_w[j], ssd_conv_b[j])
            ydir = _ssd_scan(geom, xbcc, dt, ssd_dt_bias[j], ssd_a_log[j])
            dskip_x = jnp.repeat(ssd_d[j], SSD_HEADDIM).reshape(1, SSD_INNER)
            xs, h2, route, counts = _ssd_out(geom, ydir, xbcc, z, xs, modrows, dskip_x,
                                     ssd_norm_g[j].reshape(1, SSD_INNER),
                                     ssd_w_out[j].astype(BF16), g2, rw_hi, rw_lo, rb)
        else:
            lam_init = 0.8 - 0.6 * math.exp(-0.3 * i)
            w_in = da_w_in[j]
            wqt = _head_major(w_in[:, :D_MODEL]).T.astype(BF16)
            wk = _head_major(w_in[:, D_MODEL:2 * D_MODEL]).astype(BF16)
            wvt = w_in[:, 2 * D_MODEL:].T.astype(BF16)
            q_scale = DA_HEAD_DIM ** -0.5 * LOG2E
            gqx = jnp.broadcast_to((da_q_norm[j].reshape(LANES) * q_scale)[:, None], (LANES, geom.tm))
            gk = jnp.tile(da_k_norm[j].reshape(2 * DA_HEAD_DIM), DA_HEADS).reshape(1, d)
            qt, k, vt = _da_inproj(geom, xs, modrows, g1, wqt, wk, wvt, gqx, gk, cos_t, sin_t,
                                   gsum, gexp)
            sub_gx = jnp.broadcast_to((da_sub_norm[j] * (1.0 - lam_init))[:, None],
                                      (DA_V_DIM, geom.tm))
            o = _da_attn(geom, qt, k, vt, da_lam_q[j], da_lam_k[j], sub_gx, lam_init)
            xs, h2, route, counts = _da_out(geom, o, xs, modrows, da_w_out[j].astype(BF16), g2,
                                    rw_hi, rw_lo, rb)
        plan = _moe_plan(geom, route, counts)
        y_sorted = _moe_experts(geom, h2, plan, moe_w_gate[i].astype(BF16),
                                moe_w_up[i].astype(BF16), moe_w_down[i].astype(BF16))
        xs = _moe_combine(geom, xs, modrows, y_sorted, plan[4], latent_only=(i == depth - 1))

    return xs.reshape(batch, n_seq, d)
```

```python
import functools
import math

import jax
import jax.numpy as jnp
from jax import lax
from jax.experimental import pallas as pl
from jax.experimental.pallas import tpu as pltpu

F32 = jnp.float32
BF16 = jnp.bfloat16

D_MODEL = 1024
DEPTH = 4
GRID_W = 64
RMS_EPS = 1e-6

SSD_INNER = 2048
SSD_HEADDIM = 64
SSD_HEADS = 32
SSD_GROUPS = 4
SSD_STATE = 128
SSD_CONV = 5
SSD_CHUNK = 128
SSD_BC = SSD_GROUPS * SSD_STATE
SSD_CONV_DIM = SSD_INNER + 2 * SSD_BC

DA_HEADS = 8
DA_HEAD_DIM = 64
DA_V_DIM = 128
ROPE_BASE = 10000.0
ACC_ROWS = DA_V_DIM + 16

N_EXPERTS = 16
N_GROUPS = 4
EXPERTS_PER_GROUP = 4
D_EXPERT = 512
PAIRS_PER_GROUP = 6
N_COMBOS = N_GROUPS * PAIRS_PER_GROUP
PAIR_LO = (0, 0, 0, 1, 1, 2)
PAIR_HI = (1, 2, 3, 2, 3, 3)
ROUTE_ROWS = 32
MOE_ROW = D_MODEL + 128

LANES = 128
MOD_ROWS = 16
VMEM_LIMIT = 56 * 1024 * 1024

LOG2E = 1.4426950408889634


def _cparams(sem):
    return pltpu.CompilerParams(dimension_semantics=sem, vmem_limit_bytes=VMEM_LIMIT)


def _split_bf16(a, n):
    parts = []
    r = a
    for _ in range(n):
        p = r.astype(BF16)
        parts.append(p)
        r = r - p.astype(F32)
    return parts


def _dot_split(a, b_bf16, n, dims=(((1,), (0,)), ((), ()))):
    out = None
    for p in _split_bf16(a, n):
        t = lax.dot_general(p, b_bf16, dims, preferred_element_type=F32)
        out = t if out is None else out + t
    return out


_NT = (((1,), (1,)), ((), ()))
_TN = (((0,), (0,)), ((), ()))


def _sigmoid(x):
    return 1.0 / (1.0 + jnp.exp(-x))


def _silu(x):
    return x * _sigmoid(x)


def _rms_scale(x, n):
    return lax.rsqrt(jnp.sum(x * x, axis=-1, keepdims=True) * (1.0 / n) + RMS_EPS)


def _norm_mod(x, g, shift, scale):
    y = x * _rms_scale(x, x.shape[-1]) * g
    return y * (1.0 + scale) + shift


def _ada_kernel(c_ref, w_ref, b_ref, o_ref):
    s = _silu(c_ref[...])
    acc = jnp.dot(s, w_ref[0], precision=lax.Precision.HIGHEST, preferred_element_type=F32)
    o_ref[0] = acc + b_ref[0]


def _ada_mod(cvecs, ada_w, ada_b):
    depth, d, n6 = ada_w.shape
    tn = 1536
    return pl.pallas_call(
        _ada_kernel,
        grid=(depth, n6 // tn),
        in_specs=[pl.BlockSpec((MOD_ROWS, d), lambda i, j: (0, 0)),
                  pl.BlockSpec((1, d, tn), lambda i, j: (i, 0, j)),
                  pl.BlockSpec((1, 1, tn), lambda i, j: (i, 0, j))],
        out_specs=pl.BlockSpec((1, MOD_ROWS, tn), lambda i, j: (i, 0, j)),
        out_shape=jax.ShapeDtypeStruct((depth, MOD_ROWS, n6), F32),
        compiler_params=_cparams(("arbitrary", "arbitrary")),
        name="ada_mod",
    )(cvecs, ada_w, ada_b.reshape(depth, 1, n6))


class _Geom:
    def __init__(self, batch, n_ctx, n_seq):
        self.batch, self.n_ctx, self.n_seq = batch, n_ctx, n_seq
        self.nt = n_ctx + n_seq
        self.t = batch * self.nt
        self.tm = 256 if (n_ctx % 256 == 0 and n_seq % 256 == 0) else 128
        self.tiles_per_batch = self.nt // self.tm
        self.ctx_tiles = n_ctx // self.tm
        self.n_tiles = self.t // self.tm

    def mod_index(self, i):
        b = i // self.tiles_per_batch
        r = i % self.tiles_per_batch
        return 2 * b + (r >= self.ctx_tiles).astype(jnp.int32)


def _mod_spec(geom):
    return pl.BlockSpec((1, 1, 6 * D_MODEL), lambda i: (geom.mod_index(i), 0, 0))


def _row_spec(geom, width):
    return pl.BlockSpec((geom.tm, width), lambda i: (i, 0))


def _full_spec(shape):
    return pl.BlockSpec(shape, lambda i: (0,) * len(shape))


def _chunked_dot_store(h, w_ref, o_ref, chunk=512):
    n = w_ref.shape[1]
    for j in range(0, n, chunk):
        c = min(chunk, n - j)
        o_ref[:, j:j + c] = jnp.dot(h, w_ref[:, j:j + c],
                                    preferred_element_type=F32).astype(o_ref.dtype)


def _ssd_inproj_kernel(x_ref, mod_ref, g_ref, wz_ref, wx_ref, wdt_ref, z_ref, xbc_ref, dt_ref):
    d = D_MODEL
    mod = mod_ref[0]
    h = _norm_mod(x_ref[...], g_ref[...], mod[:, 0:d], mod[:, d:2 * d]).astype(BF16)
    _chunked_dot_store(h, wz_ref, z_ref)
    _chunked_dot_store(h, wx_ref, xbc_ref)
    dt_ref[...] = jnp.dot(h, wdt_ref[...], preferred_element_type=F32)


def _ssd_inproj(geom, x, modrows, g, wz, wx, wdt):
    t = geom.t
    return pl.pallas_call(
        _ssd_inproj_kernel,
        grid=(geom.n_tiles,),
        in_specs=[_row_spec(geom, D_MODEL), _mod_spec(geom), _full_spec((1, D_MODEL)),
                  _full_spec(wz.shape), _full_spec(wx.shape), _full_spec(wdt.shape)],
        out_specs=[_row_spec(geom, SSD_INNER), _row_spec(geom, SSD_CONV_DIM),
                   _row_spec(geom, 2 * SSD_HEADS)],
        out_shape=[jax.ShapeDtypeStruct((t, SSD_INNER), BF16),
                   jax.ShapeDtypeStruct((t, SSD_CONV_DIM), BF16),
                   jax.ShapeDtypeStruct((t, 2 * SSD_HEADS), F32)],
        compiler_params=_cparams(("arbitrary",)),
        name="ssd_inproj",
    )(x, modrows, g, wz, wx, wdt)


_CONV_PAD = 8


def _ssd_conv_kernel(x_ref, w_ref, b_ref, o_ref, pad_ref, *, segments, rows):
    half = SSD_CONV // 2
    w = w_ref[...]
    bias = b_ref[...]
    cols = x_ref.shape[2]
    zeros = jnp.zeros((_CONV_PAD, cols), F32)
    for start, length in segments:
        pad_ref[0:_CONV_PAD, :] = zeros
        pad_ref[_CONV_PAD + length:2 * _CONV_PAD + length, :] = zeros
        for r in range(0, length, rows):
            pad_ref[_CONV_PAD + r:_CONV_PAD + r + rows, :] = (
                x_ref[0, start + r:start + r + rows, :].astype(F32))
        for r in range(0, length, rows):
            acc = None
            for k in range(SSD_CONV):
                lo = _CONV_PAD + r + k - half
                term = pad_ref[lo:lo + rows, :] * w[k:k + 1, :]
                acc = term if acc is None else acc + term
            o_ref[0, start + r:start + r + rows, :] = _silu(acc + bias).astype(o_ref.dtype)


def _ssd_conv(geom, xbc, conv_w, conv_b):
    cols = 512
    segments = ((0, geom.n_ctx), (geom.n_ctx, geom.n_seq))
    kern = functools.partial(_ssd_conv_kernel, segments=segments, rows=geom.tm)
    xbc3 = xbc.reshape(geom.batch, geom.nt, SSD_CONV_DIM)
    out = pl.pallas_call(
        kern,
        grid=(geom.batch, SSD_CONV_DIM // cols),
        in_specs=[pl.BlockSpec((1, geom.nt, cols), lambda b, j: (b, 0, j)),
                  pl.BlockSpec((SSD_CONV, cols), lambda b, j: (0, j)),
                  pl.BlockSpec((1, cols), lambda b, j: (0, j))],
        out_specs=pl.BlockSpec((1, geom.nt, cols), lambda b, j: (b, 0, j)),
        out_shape=jax.ShapeDtypeStruct(xbc3.shape, BF16),
        scratch_shapes=[pltpu.VMEM((max(geom.n_ctx, geom.n_seq) + 2 * _CONV_PAD, cols), F32)],
        compiler_params=_cparams(("arbitrary", "arbitrary")),
        name="ssd_conv",
    )(xbc3, conv_w, conv_b.reshape(1, SSD_CONV_DIM))
    return out


def _ssd_scan_kernel(xbc_ref, dt_ref, bias_ref, alog_ref, expand_ref, o_ref, state_ref):
    k = pl.program_id(1)
    j = pl.program_id(2)
    L, H, P, G, N = SSD_CHUNK, SSD_HEADS, SSD_HEADDIM, SSD_GROUPS, SSD_STATE
    hg = H // G

    @pl.when(j == 0)
    def _():
        state_ref[...] = jnp.zeros_like(state_ref)

    fwd = k == 0
    dt_all = dt_ref[0]
    dt_raw = jnp.where(fwd, dt_all[:, :H], dt_all[:, H:])
    bias = jnp.where(fwd, bias_ref[0:1, :], bias_ref[1:2, :])
    alog = jnp.where(fwd, alog_ref[0:1, :], alog_ref[1:2, :])
    v = dt_raw + bias
    dtk = jnp.maximum(v, 0.0) + jnp.log(1.0 + jnp.exp(-jnp.abs(v)))
    a = -jnp.exp(alog) * dtk

    row = lax.broadcasted_iota(jnp.int32, (L, L), 0)
    col = lax.broadcasted_iota(jnp.int32, (L, L), 1)
    ahead = jnp.where(fwd, row - col, col - row)
    incl = ahead >= 0
    tri = incl.astype(BF16)
    tri_t = (ahead <= 0).astype(BF16)
    a_cs = None
    a_cs_t = None
    for p in _split_bf16(a, 3):
        t1 = jnp.dot(tri, p, preferred_element_type=F32)
        t2 = lax.dot_general(p, tri_t, _TN, preferred_element_type=F32)
        a_cs = t1 if a_cs is None else a_cs + t1
        a_cs_t = t2 if a_cs_t is None else a_cs_t + t2

    expand = expand_ref[...]
    dt_x = _dot_split(dtk, expand, 2)
    acs_x = _dot_split(a_cs, expand, 2)
    tot_x = jnp.where(fwd, acs_x[L - 1:L, :], acs_x[0:1, :])

    xs = xbc_ref[0, :, 0:SSD_INNER].astype(F32)
    bm = xbc_ref[0, :, SSD_INNER:SSD_INNER + SSD_BC]
    cm = xbc_ref[0, :, SSD_INNER + SSD_BC:SSD_INNER + 2 * SSD_BC]
    xdt = xs * dt_x
    xdt_b = xdt.astype(BF16)
    z_b = (xdt * jnp.exp(tot_x - acs_x)).astype(BF16)
    decay_in = jnp.exp(acs_x)
    state = state_ref[...]
    state_b = state.astype(BF16)
    lane = lax.broadcasted_iota(jnp.int32, (L, 2 * P), 1)
    first_head = lane < P

    new_state = []
    for g in range(G):
        bg = bm[:, g * N:(g + 1) * N]
        cg = cm[:, g * N:(g + 1) * N]
        cb = lax.dot_general(cg, bg, _NT, preferred_element_type=F32)
        cols = slice(g * hg * P, (g + 1) * hg * P)
        y_off = jnp.dot(cg, state_b[:, cols], preferred_element_type=F32) * decay_in[:, cols]
        for hp in range(hg // 2):
            h0 = g * hg + 2 * hp
            gs = []
            for h in (h0, h0 + 1):
                seg = jnp.exp(jnp.where(incl, a_cs[:, h:h + 1] - a_cs_t[h:h + 1, :], -jnp.inf))
                gs.append((cb * seg).astype(BF16))
            lhs = jnp.concatenate(gs, axis=1)
            xp = xdt_b[:, h0 * P:(h0 + 2) * P]
            zero = jnp.zeros_like(xp)
            rhs = jnp.concatenate([jnp.where(first_head, xp, zero),
                                   jnp.where(first_head, zero, xp)], axis=0)
            y_pair = jnp.dot(lhs, rhs, preferred_element_type=F32)
            lo = 2 * hp * P
            o_ref[0, 0, :, h0 * P:(h0 + 2) * P] = (y_pair + y_off[:, lo:lo + 2 * P]).astype(o_ref.dtype)
        upd = lax.dot_general(bg, z_b[:, cols], _TN, preferred_element_type=F32)
        new_state.append(state[:, cols] * jnp.exp(tot_x[:, cols]) + upd)
    for g in range(G):
        state_ref[:, g * hg * P:(g + 1) * hg * P] = new_state[g]


def _ssd_scan(geom, xbcc, dt, dt_bias, a_log):
    nch = geom.nt // SSD_CHUNK
    cch = geom.n_ctx // SSD_CHUNK

    def chunk_index(k, j):
        rev = jnp.where(j < cch, cch - 1 - j, nch - 1 - (j - cch))
        return jnp.where(k == 0, j, rev)

    expand = (jnp.arange(SSD_HEADS)[:, None] == (jnp.arange(SSD_INNER)[None, :] // SSD_HEADDIM)).astype(BF16)
    dt3 = dt.reshape(geom.batch, geom.nt, 2 * SSD_HEADS)
    return pl.pallas_call(
        _ssd_scan_kernel,
        grid=(geom.batch, 2, nch),
        in_specs=[pl.BlockSpec((1, SSD_CHUNK, SSD_CONV_DIM), lambda b, k, j: (b, chunk_index(k, j), 0)),
                  pl.BlockSpec((1, SSD_CHUNK, 2 * SSD_HEADS), lambda b, k, j: (b, chunk_index(k, j), 0)),
                  pl.BlockSpec((2, SSD_HEADS), lambda b, k, j: (0, 0)),
                  pl.BlockSpec((2, SSD_HEADS), lambda b, k, j: (0, 0)),
                  pl.BlockSpec((SSD_HEADS, SSD_INNER), lambda b, k, j: (0, 0))],
        out_specs=pl.BlockSpec((1, 1, SSD_CHUNK, SSD_INNER), lambda b, k, j: (k, b, chunk_index(k, j), 0)),
        out_shape=jax.ShapeDtypeStruct((2, geom.batch, geom.nt, SSD_INNER), BF16),
        scratch_shapes=[pltpu.VMEM((SSD_STATE, SSD_INNER), F32)],
        compiler_params=_cparams(("arbitrary", "arbitrary", "arbitrary")),
        name="ssd_scan",
    )(xbcc, dt3, dt_bias, a_log, expand)


def _route(h2, rwhi_ref, rwlo_ref, rb_ref, rt_ref, cnt_out_ref, cnt_ref):
    tm = h2.shape[0]
    h_hi, h_lo = _split_bf16(h2, 2)
    rw_hi, rw_lo = rwhi_ref[...], rwlo_ref[...]
    logits = (lax.dot_general(rw_hi, h_hi, _NT, preferred_element_type=F32)
              + lax.dot_general(rw_hi, h_lo, _NT, preferred_element_type=F32)
              + lax.dot_general(rw_lo, h_hi, _NT, preferred_element_type=F32))
    scores = _sigmoid(logits)
    biased = scores + rb_ref[...]
    s = [scores[e:e + 1, :] for e in range(N_EXPERTS)]
    b = [biased[e:e + 1, :] for e in range(N_EXPERTS)]

    def pair_max(v):
        m = v[0] + v[1]
        for lo, hi in zip(PAIR_LO[1:], PAIR_HI[1:]):
            m = jnp.maximum(m, v[lo] + v[hi])
        return m

    gsc = [pair_max(b[4 * g:4 * g + 4]) for g in range(N_GROUPS)]
    gbest = jnp.zeros((1, tm), jnp.int32)
    best = gsc[0]
    for g in range(1, N_GROUPS):
        take = gsc[g] > best
        gbest = jnp.where(take, g, gbest)
        best = jnp.where(take, gsc[g], best)

    def pick(vals, j):
        out = vals[j]
        for g in range(1, N_GROUPS):
            out = jnp.where(gbest == g, vals[4 * g + j], out)
        return out

    vb = [pick(b, j) for j in range(EXPERTS_PER_GROUP)]
    vs = [pick(s, j) for j in range(EXPERTS_PER_GROUP)]

    def argmax4(v):
        idx = jnp.zeros((1, tm), jnp.int32)
        m = v[0]
        for j in range(1, EXPERTS_PER_GROUP):
            take = v[j] > m
            idx = jnp.where(take, j, idx)
            m = jnp.where(take, v[j], m)
        return idx

    i1 = argmax4(vb)
    i2 = argmax4([jnp.where(i1 == j, -jnp.inf, vb[j]) for j in range(EXPERTS_PER_GROUP)])
    s1 = sum(jnp.where(i1 == j, vs[j], 0.0) for j in range(EXPERTS_PER_GROUP))
    s2 = sum(jnp.where(i2 == j, vs[j], 0.0) for j in range(EXPERTS_PER_GROUP))
    w1 = s1 / (s1 + s2)
    w2 = s2 / (s1 + s2)
    lo = jnp.minimum(i1, i2)
    hi = jnp.maximum(i1, i2)
    pair = jnp.where(lo == 0, hi - 1, jnp.where(lo == 1, hi + 1, 5))
    combo = gbest * PAIRS_PER_GROUP + pair
    first_lo = i1 < i2
    w_lo = jnp.where(first_lo, w1, w2)
    w_hi = jnp.where(first_lo, w2, w1)

    @pl.when(pl.program_id(0) == 0)
    def _():
        cnt_ref[...] = jnp.zeros(cnt_ref.shape, F32)

    crow = lax.broadcasted_iota(jnp.int32, (ROUTE_ROWS, tm), 0)
    onehot = (crow == combo).astype(F32)
    src = lax.broadcasted_iota(jnp.int32, (tm, tm), 0)
    dst = lax.broadcasted_iota(jnp.int32, (tm, tm), 1)
    upper = (src <= dst).astype(BF16)
    incl = jnp.dot(onehot.astype(BF16), upper, preferred_element_type=F32)
    cnt = cnt_ref[...]
    base = jnp.concatenate([cnt] * (tm // LANES), axis=1)
    rank = jnp.sum(onehot * (incl + base), axis=0, keepdims=True) - 1.0
    cnt_new = cnt + jnp.sum(onehot, axis=1, keepdims=True)
    cnt_ref[...] = cnt_new
    cnt_out_ref[...] = cnt_new

    rt_ref[0:1, :] = combo.astype(F32)
    rt_ref[1:2, :] = rank
    rt_ref[2:8, :] = jnp.zeros((6, tm), F32)
    irow = lax.broadcasted_iota(jnp.int32, (LANES, tm), 0)
    info = jnp.where(irow == 0, w_lo, jnp.where(irow == 1, w_hi, 0.0))
    return info.T


def _residual_route(x, y, mod, n2g, rwhi_ref, rwlo_ref, rb_ref, xo_ref, h2_ref, rt_ref,
                    cnt_out_ref, cnt_ref):
    d = D_MODEL
    x_new = x + mod[:, 2 * d:3 * d] * y
    xo_ref[...] = x_new
    h2 = _norm_mod(x_new, n2g, mod[:, 3 * d:4 * d], mod[:, 4 * d:5 * d])
    h2_ref[:, 0:d] = h2
    h2_ref[:, d:d + LANES] = _route(h2, rwhi_ref, rwlo_ref, rb_ref, rt_ref, cnt_out_ref, cnt_ref)


def _ssd_out_kernel(yd_ref, xs_ref, z_ref, x_ref, mod_ref, dsk_ref, ng_ref, w_ref, n2g_ref,
                    rwhi_ref, rwlo_ref, rb_ref, xo_ref, h2_ref, rt_ref, cnt_out_ref, cnt_ref):
    y = dsk_ref[...] * xs_ref[...].astype(F32) + yd_ref[0].astype(F32) + yd_ref[1].astype(F32)
    y = y * _silu(z_ref[...].astype(F32))
    yn = (y * _rms_scale(y, SSD_INNER) * ng_ref[...]).astype(BF16)
    out = jnp.dot(yn, w_ref[...], preferred_element_type=F32)
    _residual_route(x_ref[...], out, mod_ref[0], n2g_ref[...], rwhi_ref, rwlo_ref, rb_ref,
                    xo_ref, h2_ref, rt_ref, cnt_out_ref, cnt_ref)


def _route_specs(geom):
    in_specs = [_full_spec((1, D_MODEL)), _full_spec((N_EXPERTS, D_MODEL)),
                _full_spec((N_EXPERTS, D_MODEL)), _full_spec((N_EXPERTS, 1))]
    out_specs = [_row_spec(geom, D_MODEL), _row_spec(geom, MOE_ROW),
                 pl.BlockSpec((8, geom.tm), lambda i: (i, 0)),
                 _full_spec((ROUTE_ROWS, LANES))]
    out_shape = [jax.ShapeDtypeStruct((geom.t, D_MODEL), F32),
                 jax.ShapeDtypeStruct((geom.t, MOE_ROW), F32),
                 jax.ShapeDtypeStruct((geom.n_tiles * 8, geom.tm), F32),
                 jax.ShapeDtypeStruct((ROUTE_ROWS, LANES), F32)]
    scratch = [pltpu.VMEM((ROUTE_ROWS, LANES), F32)]
    return in_specs, out_specs, out_shape, scratch


def _ssd_out(geom, ydir, xbcc, z, x, modrows, dskip_x, norm_g, w_out, n2g, rw_hi, rw_lo, rb):
    r_in, r_out, r_shape, r_scratch = _route_specs(geom)
    tm = geom.tm
    yd2 = ydir.reshape(2, geom.t, SSD_INNER)
    xbcc2 = xbcc.reshape(geom.t, SSD_CONV_DIM)
    return pl.pallas_call(
        _ssd_out_kernel,
        grid=(geom.n_tiles,),
        in_specs=[pl.BlockSpec((2, tm, SSD_INNER), lambda i: (0, i, 0)),
                  pl.BlockSpec((tm, SSD_INNER), lambda i: (i, 0)),
                  _row_spec(geom, SSD_INNER), _row_spec(geom, D_MODEL), _mod_spec(geom),
                  _full_spec((1, SSD_INNER)), _full_spec((1, SSD_INNER)),
                  _full_spec(w_out.shape)] + r_in,
        out_specs=r_out,
        out_shape=r_shape,
        scratch_shapes=r_scratch,
        compiler_params=_cparams(("arbitrary",)),
        name="ssd_out",
    )(yd2, xbcc2, z, x, modrows, dskip_x, norm_g, w_out, n2g, rw_hi, rw_lo, rb)


def _da_inproj_kernel(x_ref, mod_ref, g_ref, wqt_ref, wk_ref, wvt_ref, gqx_ref, gk_ref,
                      cos_ref, sin_ref, cost_ref, sint_ref, gsum_ref, gexp_ref,
                      qt_ref, k_ref, vt_ref):
    d = D_MODEL
    mod = mod_ref[0]
    h = _norm_mod(x_ref[...], g_ref[...], mod[:, 0:d], mod[:, d:2 * d]).astype(BF16)
    tm = h.shape[0]

    vt_ref[0, 0] = lax.dot_general(wvt_ref[...], h, _NT, preferred_element_type=F32).astype(BF16)

    yt = lax.dot_general(wqt_ref[...], h, _NT, preferred_element_type=F32)
    cost = cost_ref[...]
    sint = sint_ref[...]
    gqx = gqx_ref[...]
    for hd in range(DA_HEADS):
        y3 = yt[hd * LANES:(hd + 1) * LANES, :].reshape(2, DA_HEAD_DIM, tm)
        ss = jnp.sum(y3 * y3, axis=1, keepdims=True)
        yn = (y3 * lax.rsqrt(ss * (1.0 / DA_HEAD_DIM) + RMS_EPS)).reshape(LANES, tm) * gqx
        y4 = yn.reshape(4, 2, 16, tm)
        partner = jnp.concatenate([y4[:, 1:2], y4[:, 0:1]], axis=1).reshape(LANES, tm)
        qt_ref[0, 0, hd * LANES:(hd + 1) * LANES, :] = (yn * cost + partner * sint).astype(BF16)

    cos = cos_ref[...]
    sin = sin_ref[...]
    lane = lax.broadcasted_iota(jnp.int32, cos.shape, 1)
    first_half = (lane % 32) < 16
    y = jnp.dot(h, wk_ref[...], preferred_element_type=F32)
    ss = jnp.dot((y * y).astype(BF16), gsum_ref[...], preferred_element_type=F32)
    r = lax.rsqrt(ss * (1.0 / DA_HEAD_DIM) + RMS_EPS)
    rx = _dot_split(r, gexp_ref[...], 2)
    yn = y * rx * gk_ref[...]
    for hd in range(DA_HEADS):
        c = yn[:, hd * LANES:(hd + 1) * LANES]
        partner = jnp.where(first_half, pltpu.roll(c, LANES - 16, 1), pltpu.roll(c, 16, 1))
        k_ref[:, hd * LANES:(hd + 1) * LANES] = (c * cos + partner * sin).astype(k_ref.dtype)


def _da_inproj(geom, x, modrows, g, wqt, wk, wvt, gqx, gk, cos_t, sin_t, gsum, gexp):
    t, tm = geom.t, geom.tm
    tpb = geom.tiles_per_batch
    tab_spec = pl.BlockSpec((tm, LANES), lambda i: (i % tpb, 0))
    tabt_spec = pl.BlockSpec((LANES, tm), lambda i: (0, i % tpb))
    tr_spec = pl.BlockSpec((1, 1, D_MODEL, tm), lambda i: (i // tpb, i % tpb, 0, 0))
    tr_shape = jax.ShapeDtypeStruct((geom.batch, tpb, D_MODEL, tm), BF16)
    return pl.pallas_call(
        _da_inproj_kernel,
        grid=(geom.n_tiles,),
        in_specs=[_row_spec(geom, D_MODEL), _mod_spec(geom), _full_spec((1, D_MODEL)),
                  _full_spec(wqt.shape), _full_spec(wk.shape), _full_spec(wvt.shape),
                  _full_spec((LANES, tm)), _full_spec((1, D_MODEL)), tab_spec, tab_spec,
                  tabt_spec, tabt_spec, _full_spec(gsum.shape), _full_spec(gexp.shape)],
        out_specs=[tr_spec, _row_spec(geom, D_MODEL), tr_spec],
        out_shape=[tr_shape, jax.ShapeDtypeStruct((t, D_MODEL), BF16), tr_shape],
        compiler_params=_cparams(("arbitrary",)),
        name="da_inproj",
    )(x, modrows, g, wqt, wk, wvt, gqx, gk, cos_t, sin_t, cos_t.T, sin_t.T, gsum, gexp)


def _da_attn_kernel(qt_ref, k_ref, vt_ref, lq_ref, lk_ref, sgx_ref, o_ref,
                    s0_ref, s1_ref, acc_ref, p_ref, *, ctx_tiles, all_tiles, lam_init, unroll, heads):
    qi = pl.program_id(2)
    tk = vt_ref.shape[3]
    tq = qt_ref.shape[3]
    n_str = 2 * heads
    row = lax.broadcasted_iota(jnp.int32, (LANES, tq), 0)
    qm = []
    for hh in range(heads):
        qt = qt_ref[0, 0, hh * LANES:(hh + 1) * LANES, :]
        zero = jnp.zeros_like(qt)
        qm += [jnp.where(row < DA_HEAD_DIM, qt, zero), jnp.where(row < DA_HEAD_DIM, zero, qt)]
    n_chunks = jnp.where(qi < ctx_tiles, ctx_tiles, all_tiles)
    acc_ref[...] = jnp.zeros(acc_ref.shape, F32)

    def scores(j, s_ref):
        off = pl.multiple_of(j * tk, tk)
        kc = k_ref[0, pl.ds(off, tk), :]
        for st in range(n_str):
            hh = st // 2
            s_ref[st] = jnp.dot(kc[:, hh * LANES:(hh + 1) * LANES], qm[st],
                                preferred_element_type=F32).astype(BF16)

    def softmax(s_ref, stats):
        alphas, pts, new_stats = [], [], []
        for st in range(n_str):
            m_old = stats[st]
            sb = s_ref[st]
            m_new = jnp.maximum(m_old, jnp.max(sb, axis=0, keepdims=True).astype(F32))
            alphas.append(jnp.exp2(m_old - m_new))
            pts.append(jnp.exp2(sb - m_new.astype(BF16)))
            new_stats.append(m_new)
        return tuple(alphas), tuple(pts), tuple(new_stats)

    ones_rows = jnp.ones((ACC_ROWS - DA_V_DIM, tk), BF16)

    def attend(j, alphas, pts):
        vt_all = vt_ref[0, j]
        for hh in range(heads):
            vtc = jnp.concatenate([vt_all[hh * LANES:(hh + 1) * LANES], ones_rows], axis=0)
            for st in (2 * hh, 2 * hh + 1):
                acc_ref[st] = alphas[st] * acc_ref[st] + jnp.dot(vtc, pts[st],
                                                                 preferred_element_type=F32)

    scores(0, s0_ref)
    p_ref[...] = jnp.zeros(p_ref.shape, BF16)
    s_refs = (s0_ref, s1_ref)

    def body(i, carry):
        stats, alpha_pend = carry
        a = unroll * i
        prev_j, prev_alpha = jnp.maximum(a - 1, 0), alpha_pend
        prev_p = tuple(p_ref[st] for st in range(n_str))
        for u in range(unroll):
            scores(a + u + 1, s_refs[(u + 1) % 2])
            alphas, pts, stats = softmax(s_refs[u % 2], stats)
            attend(prev_j, prev_alpha, prev_p)
            prev_j, prev_alpha, prev_p = a + u, alphas, pts
        for st in range(n_str):
            p_ref[st] = prev_p[st]
        return stats, prev_alpha

    neg = jnp.full((1, tq), -1e30, F32)
    one = jnp.ones((1, tq), F32)
    stats, alpha_pend = lax.fori_loop(0, (n_chunks - 1) // unroll, body,
                                      ((neg,) * n_str, (one,) * n_str))
    a_last, p_last, stats = softmax(s0_ref, stats)
    attend(jnp.maximum(n_chunks - 2, 0), alpha_pend, tuple(p_ref[st] for st in range(n_str)))
    attend(n_chunks - 1, a_last, p_last)

    tdot = jnp.sum(lq_ref[...] * lk_ref[...], axis=-1, keepdims=True)
    e = jnp.exp(tdot)
    lam = e[0:1, :] - e[1:2, :] + lam_init
    dv = DA_V_DIM
    for hh in range(heads):
        a1, a2 = acc_ref[2 * hh], acc_ref[2 * hh + 1]
        ot = a1[0:dv, :] / a1[dv:dv + 1, :] - lam * (a2[0:dv, :] / a2[dv:dv + 1, :])
        ms = jnp.sum(ot * ot, axis=0, keepdims=True) * (1.0 / DA_V_DIM)
        ot = ot * lax.rsqrt(ms + RMS_EPS) * sgx_ref[...]
        o_ref[0, :, hh * LANES:(hh + 1) * LANES] = ot.T.astype(o_ref.dtype)


def _da_attn(geom, qt, k, vt, lam_q, lam_k, sub_gx, lam_init):
    tq = geom.tm
    b, nt, tpb = geom.batch, geom.nt, geom.tiles_per_batch
    k3 = k.reshape(b, nt, D_MODEL)
    heads = 4
    hw = heads * LANES
    unroll = 4 if ((tpb - 1) % 4 == 0 and (geom.ctx_tiles - 1) % 4 == 0) else 2
    assert (tpb - 1) % unroll == 0 and (geom.ctx_tiles - 1) % unroll == 0
    kern = functools.partial(_da_attn_kernel, ctx_tiles=geom.ctx_tiles, all_tiles=tpb,
                             lam_init=lam_init, unroll=unroll, heads=heads)
    small = lambda shape: pl.BlockSpec(shape, lambda bi, h, i: (0, 0))
    return pl.pallas_call(
        kern,
        grid=(b, DA_HEADS // heads, tpb),
        in_specs=[pl.BlockSpec((1, 1, hw, tq), lambda bi, h, i: (bi, i, h, 0)),
                  pl.BlockSpec((1, nt, hw), lambda bi, h, i: (bi, 0, h)),
                  pl.BlockSpec((1, tpb, hw, tq), lambda bi, h, i: (bi, 0, h, 0)),
                  small((2, DA_HEAD_DIM)), small((2, DA_HEAD_DIM)), small((DA_V_DIM, tq))],
        out_specs=pl.BlockSpec((1, tq, hw), lambda bi, h, i: (bi, i, h)),
        out_shape=jax.ShapeDtypeStruct((b, nt, D_MODEL), BF16),
        scratch_shapes=[pltpu.VMEM((2 * heads, tq, tq), BF16), pltpu.VMEM((2 * heads, tq, tq), BF16),
                        pltpu.VMEM((2 * heads, ACC_ROWS, tq), F32),
                        pltpu.VMEM((2 * heads, tq, tq), BF16)],
        compiler_params=_cparams(("arbitrary", "arbitrary", "arbitrary")),
        name="da_attn",
    )(qt, k3, vt, lam_q, lam_k, sub_gx)


def _da_out_kernel(o_ref, x_ref, mod_ref, w_ref, n2g_ref, rwhi_ref, rwlo_ref, rb_ref,
                   xo_ref, h2_ref, rt_ref, cnt_out_ref, cnt_ref):
    out = jnp.dot(o_ref[...], w_ref[...], preferred_element_type=F32)
    _residual_route(x_ref[...], out, mod_ref[0], n2g_ref[...], rwhi_ref, rwlo_ref, rb_ref,
                    xo_ref, h2_ref, rt_ref, cnt_out_ref, cnt_ref)


def _da_out(geom, o, x, modrows, w_out, n2g, rw_hi, rw_lo, rb):
    r_in, r_out, r_shape, r_scratch = _route_specs(geom)
    return pl.pallas_call(
        _da_out_kernel,
        grid=(geom.n_tiles,),
        in_specs=[_row_spec(geom, D_MODEL), _row_spec(geom, D_MODEL), _mod_spec(geom),
                  _full_spec(w_out.shape)] + r_in,
        out_specs=r_out,
        out_shape=r_shape,
        scratch_shapes=r_scratch,
        compiler_params=_cparams(("arbitrary",)),
        name="da_out",
    )(o.reshape(geom.t, D_MODEL), x, modrows, w_out, n2g, rw_hi, rw_lo, rb)


def _gather_rows(idx_ref, base, src_hbm, dst, sem, rows):
    def body(r, carry):
        t = idx_ref[base + r]
        pltpu.make_async_copy(src_hbm.at[pl.ds(t, 1), :], dst.at[pl.ds(r, 1), :], sem).start()
        return carry
    lax.fori_loop(0, rows, body, 0, unroll=8)


def _wait_rows(src_hbm, dst, sem, rows):
    pltpu.make_async_copy(src_hbm.at[pl.ds(0, rows), :], dst, sem).wait()


def _issue_rows(idx_ref, base, src_hbm, dst, sem, rows):
    for r in range(rows):
        t = idx_ref[base + r]
        pltpu.make_async_copy(src_hbm.at[pl.ds(t, 1), :], dst.at[pl.ds(r, 1), :], sem).start()


def _moe_kernel(e0_ref, e1_ref, used_ref, tok_ref, h2_hbm,
                wg0_ref, wu0_ref, wd0_ref, wg1_ref, wu1_ref, wd1_ref, y_ref, xa, xb, sem):
    del e0_ref, e1_ref
    i = pl.program_id(0)
    tm = y_ref.shape[0]
    n_used = used_ref[0]
    even = i % 2 == 0

    @pl.when(i == 0)
    def _():
        _gather_rows(tok_ref, 0, h2_hbm, xa, sem.at[0], tm)

    def step(cur, cur_sem, nxt, nxt_sem):
        _wait_rows(h2_hbm, cur, cur_sem, tm)
        _issue_rows(tok_ref, (i + 1) * tm, h2_hbm, nxt, nxt_sem, tm)
        x = cur[:, 0:D_MODEL].astype(BF16)
        y = None
        for e, (wg, wu, wd) in enumerate(((wg0_ref, wu0_ref, wd0_ref), (wg1_ref, wu1_ref, wd1_ref))):
            gate = jnp.dot(x, wg[0], preferred_element_type=F32)
            up = jnp.dot(x, wu[0], preferred_element_type=F32)
            hid = (_silu(gate) * up).astype(BF16)
            ye = (jnp.dot(hid, wd[0], preferred_element_type=F32)
                  * cur[:, D_MODEL + e:D_MODEL + e + 1])
            y = ye if y is None else y + ye
        y_ref[...] = y

    @pl.when(jnp.logical_and(i < n_used, even))
    def _():
        step(xa, sem.at[0], xb, sem.at[1])

    @pl.when(jnp.logical_and(i < n_used, jnp.logical_not(even)))
    def _():
        step(xb, sem.at[1], xa, sem.at[0])

    @pl.when(i >= n_used)
    def _():
        y_ref[...] = jnp.zeros_like(y_ref)

    @pl.when(jnp.logical_and(i == n_used, even))
    def _():
        _wait_rows(h2_hbm, xa, sem.at[0], tm)

    @pl.when(jnp.logical_and(i == n_used, jnp.logical_not(even)))
    def _():
        _wait_rows(h2_hbm, xb, sem.at[1], tm)


def _moe_plan(geom, route, counts_rep):
    tm, t = geom.tm, geom.t
    rt = route.reshape(geom.n_tiles, 8, tm)
    combo = rt[:, 0, :].reshape(t).astype(jnp.int32)
    rank = rt[:, 1, :].reshape(t).astype(jnp.int32)
    counts = counts_rep[:N_COMBOS, 0].astype(jnp.int32)
    padded = ((counts + tm - 1) // tm) * tm
    ends = jnp.cumsum(padded)
    offs = ends - padded
    onehot = combo[:, None] == jnp.arange(N_COMBOS, dtype=jnp.int32)[None, :]
    pos = jnp.sum(jnp.where(onehot, offs[None, :], 0), axis=1) + rank
    p_rows = t + N_COMBOS * tm
    n_ptiles = p_rows // tm
    tok_sorted = jnp.zeros((p_rows,), jnp.int32).at[pos].set(jnp.arange(t, dtype=jnp.int32))
    n_used = (ends[-1] // tm).astype(jnp.int32)
    tile_start = jnp.arange(n_ptiles, dtype=jnp.int32) * tm
    last_start = jnp.maximum(n_used - 1, 0) * tm
    tile_combo = jnp.sum(ends[None, :] <= jnp.minimum(tile_start, last_start)[:, None], axis=1)
    tile_combo = jnp.minimum(tile_combo, N_COMBOS - 1).astype(jnp.int32)
    grp = tile_combo // PAIRS_PER_GROUP
    pair = tile_combo % PAIRS_PER_GROUP
    e0 = grp * EXPERTS_PER_GROUP + jnp.asarray(PAIR_LO, jnp.int32)[pair]
    e1 = grp * EXPERTS_PER_GROUP + jnp.asarray(PAIR_HI, jnp.int32)[pair]
    return e0, e1, n_used.reshape(1), tok_sorted, pos


def _moe_experts(geom, h2, plan, wg, wu, wd):
    e0, e1, n_used, tok_sorted, _ = plan
    tm = geom.tm
    p_rows = tok_sorted.shape[0]
    de = wg.shape[2]

    def wspec(shape, which):
        if which == 0:
            return pl.BlockSpec(shape, lambda i, e0r, e1r, ur, tr: (e0r[i], 0, 0))
        return pl.BlockSpec(shape, lambda i, e0r, e1r, ur, tr: (e1r[i], 0, 0))

    grid_spec = pltpu.PrefetchScalarGridSpec(
        num_scalar_prefetch=4,
        grid=(p_rows // tm,),
        in_specs=[pl.BlockSpec(memory_space=pl.ANY),
                  wspec((1, D_MODEL, de), 0), wspec((1, D_MODEL, de), 0), wspec((1, de, D_MODEL), 0),
                  wspec((1, D_MODEL, de), 1), wspec((1, D_MODEL, de), 1), wspec((1, de, D_MODEL), 1)],
        out_specs=pl.BlockSpec((tm, D_MODEL), lambda i, *_: (i, 0)),
        scratch_shapes=[pltpu.VMEM((tm, MOE_ROW), F32), pltpu.VMEM((tm, MOE_ROW), F32),
                        pltpu.SemaphoreType.DMA((2,))],
    )
    return pl.pallas_call(
        _moe_kernel,
        grid_spec=grid_spec,
        out_shape=jax.ShapeDtypeStruct((p_rows, D_MODEL), F32),
        compiler_params=_cparams(("arbitrary",)),
        name="moe_experts",
    )(e0, e1, n_used, tok_sorted, h2, wg, wu, wd, wg, wu, wd)


def _moe_combine_kernel(pos_ref, x_ref, mod_ref, y_hbm, o_ref, ybuf, sem):
    i = pl.program_id(0)
    n = pl.num_programs(0)
    tm = o_ref.shape[0]
    slot = i % 2
    d = D_MODEL

    @pl.when(i == 0)
    def _():
        _gather_rows(pos_ref, 0, y_hbm, ybuf.at[0], sem.at[0], tm)

    @pl.when(i + 1 < n)
    def _():
        _gather_rows(pos_ref, (i + 1) * tm, y_hbm, ybuf.at[1 - slot], sem.at[1 - slot], tm)

    _wait_rows(y_hbm, ybuf.at[slot], sem.at[slot], tm)
    mod = mod_ref[0]
    o_ref[...] = x_ref[...] + mod[:, 5 * d:6 * d] * ybuf[slot]


def _moe_combine(geom, x, modrows, y_sorted, pos, latent_only):
    tm = geom.tm
    tpb, ctx_tiles = geom.tiles_per_batch, geom.ctx_tiles
    lat_tiles = tpb - ctx_tiles

    def out_index(i, p):
        if not latent_only:
            return (i, 0)
        return ((i // tpb) * lat_tiles + jnp.maximum(i % tpb - ctx_tiles, 0), 0)

    out_rows = geom.batch * geom.n_seq if latent_only else geom.t
    grid_spec = pltpu.PrefetchScalarGridSpec(
        num_scalar_prefetch=1,
        grid=(geom.n_tiles,),
        in_specs=[pl.BlockSpec((tm, D_MODEL), lambda i, p: (i, 0)),
                  pl.BlockSpec((1, 1, 6 * D_MODEL), lambda i, p: (geom.mod_index(i), 0, 0)),
                  pl.BlockSpec(memory_space=pl.ANY)],
        out_specs=pl.BlockSpec((tm, D_MODEL), out_index),
        scratch_shapes=[pltpu.VMEM((2, tm, D_MODEL), F32), pltpu.SemaphoreType.DMA((2,))],
    )
    return pl.pallas_call(
        _moe_combine_kernel,
        grid_spec=grid_spec,
        out_shape=jax.ShapeDtypeStruct((out_rows, D_MODEL), F32),
        compiler_params=_cparams(("arbitrary",)),
        name="moe_combine",
    )(pos, x, modrows, y_sorted)


def _rope_tables(geom):
    tpos = jnp.arange(geom.n_seq, dtype=jnp.int32)
    pos = jnp.stack([tpos // GRID_W, tpos % GRID_W], axis=-1).astype(F32)
    n_freq = DA_HEAD_DIM // 4
    inv_freq = ROPE_BASE ** (-jnp.arange(n_freq, dtype=F32) / n_freq)
    ang = pos[..., None] * inv_freq
    cos, sin = jnp.cos(ang), jnp.sin(ang)
    cos_l = jnp.broadcast_to(cos[:, None, :, None, :], (geom.n_seq, 2, 2, 2, n_freq))
    sgn = jnp.asarray([-1.0, 1.0], F32)[None, None, None, :, None]
    sin_l = jnp.broadcast_to(sin[:, None, :, None, :], (geom.n_seq, 2, 2, 2, n_freq)) * sgn
    cos_l = cos_l.reshape(geom.n_seq, LANES)
    sin_l = sin_l.reshape(geom.n_seq, LANES)
    cos_t = jnp.concatenate([jnp.ones((geom.n_ctx, LANES), F32), cos_l], axis=0)
    sin_t = jnp.concatenate([jnp.zeros((geom.n_ctx, LANES), F32), sin_l], axis=0)
    return cos_t, sin_t


def _head_major(w):
    return w.reshape(D_MODEL, 2, DA_HEADS, DA_HEAD_DIM).transpose(0, 2, 1, 3).reshape(D_MODEL, D_MODEL)


def kernel(x, c, ctx, c_ctx, ada_w, ada_b, norm1_g, norm2_g, ssd_w_in, ssd_conv_w, ssd_conv_b,
           ssd_dt_bias, ssd_a_log, ssd_d, ssd_norm_g, ssd_w_out, da_w_in, da_q_norm, da_k_norm,
           da_lam_q, da_lam_k, da_sub_norm, da_w_out, router_w, router_b, moe_w_gate, moe_w_up,
           moe_w_down):
    batch, n_seq, d = x.shape
    n_ctx = ctx.shape[1]
    depth = ada_w.shape[0]
    assert d == D_MODEL and batch + 1 <= MOD_ROWS
    geom = _Geom(batch, n_ctx, n_seq)

    cvecs = jnp.zeros((MOD_ROWS, d), F32).at[:batch].set(c).at[batch].set(c_ctx)
    mod_all = _ada_mod(cvecs, ada_w, ada_b)

    xs = jnp.concatenate([ctx, x], axis=1).reshape(geom.t, d)

    rw_t = router_w.T
    rw_hi = rw_t.astype(BF16)
    rw_lo = (rw_t - rw_hi.astype(F32)).astype(BF16)
    rb = router_b.reshape(N_EXPERTS, 1)

    cos_t, sin_t = _rope_tables(geom)
    grp_of_col = jnp.arange(D_MODEL) // DA_HEAD_DIM
    gsum = (grp_of_col[:, None] == jnp.arange(LANES)[None, :]).astype(BF16)
    gexp = gsum.T

    for i in range(depth):
        mod_i = mod_all[i]
        modrows = jnp.stack([jnp.broadcast_to(mod_i[batch], (batch, 6 * d)), mod_i[:batch]],
                            axis=1).reshape(2 * batch, 1, 6 * d)
        g1 = norm1_g[i].reshape(1, d)
        g2 = norm2_g[i].reshape(1, d)
        j = i // 2
        if i % 2 == 0:
            w_in = ssd_w_in[j].astype(BF16)
            wz = w_in[:, :SSD_INNER]
            wx = w_in[:, SSD_INNER:SSD_INNER + SSD_CONV_DIM]
            wdt = w_in[:, SSD_INNER + SSD_CONV_DIM:]
            z, xbc, dt = _ssd_inproj(geom, xs, modrows, g1, wz, wx, wdt)
            xbcc = _ssd_conv(geom, xbc, ssd_conv_w[j], ssd_conv_b[j])
            ydir = _ssd_scan(geom, xbcc, dt, ssd_dt_bias[j], ssd_a_log[j])
            dskip_x = jnp.repeat(ssd_d[j], SSD_HEADDIM).reshape(1, SSD_INNER)
            xs, h2, route, counts = _ssd_out(geom, ydir, xbcc, z, xs, modrows, dskip_x,
                                     ssd_norm_g[j].reshape(1, SSD_INNER),
                                     ssd_w_out[j].astype(BF16), g2, rw_hi, rw_lo, rb)
        else:
            lam_init = 0.8 - 0.6 * math.exp(-0.3 * i)
            w_in = da_w_in[j]
            wqt = _head_major(w_in[:, :D_MODEL]).T.astype(BF16)
            wk = _head_major(w_in[:, D_MODEL:2 * D_MODEL]).astype(BF16)
            wvt = w_in[:, 2 * D_MODEL:].T.astype(BF16)
            q_scale = DA_HEAD_DIM ** -0.5 * LOG2E
            gqx = jnp.broadcast_to((da_q_norm[j].reshape(LANES) * q_scale)[:, None], (LANES, geom.tm))
            gk = jnp.tile(da_k_norm[j].reshape(2 * DA_HEAD_DIM), DA_HEADS).reshape(1, d)
            qt, k, vt = _da_inproj(geom, xs, modrows, g1, wqt, wk, wvt, gqx, gk, cos_t, sin_t,
                                   gsum, gexp)
            sub_gx = jnp.broadcast_to((da_sub_norm[j] * (1.0 - lam_init))[:, None],
                                      (DA_V_DIM, geom.tm))
            o = _da_attn(geom, qt, k, vt, da_lam_q[j], da_lam_k[j], sub_gx, lam_init)
            xs, h2, route, counts = _da_out(geom, o, xs, modrows, da_w_out[j].astype(BF16), g2,
                                    rw_hi, rw_lo, rb)
        plan = _moe_plan(geom, route, counts)
        y_sorted = _moe_experts(geom, h2, plan, moe_w_gate[i].astype(BF16),
                                moe_w_up[i].astype(BF16), moe_w_down[i].astype(BF16))
        xs = _moe_combine(geom, xs, modrows, y_sorted, plan[4], latent_only=(i == depth - 1))

    return xs.reshape(batch, n_seq, d)
```
